```python
import math
import jax, jax.numpy as jnp
from jax import lax
import numpy as np

D_MODEL = 1024
BATCH = 16
SEQ = 2048
DEPTH = 2

HEAD_DIM = 64
N_DIFF = 4
DIFF_V = 2 * HEAD_DIM
N_FOX = 8
N_SB = 8
N_BRANCH = 3
D_FF = 2816
Q_BLOCK = 128
EPS = 1e-6
SUBLN_EPS = 1e-5

A_QK = N_DIFF * 2 * HEAD_DIM
W_A = N_DIFF * DIFF_V
W_B = N_FOX * HEAD_DIM
W_C = N_SB * HEAD_DIM
IN_SIZES = (A_QK, A_QK, W_A, W_B, W_B, W_B, N_FOX, W_C, W_C, W_C, N_BRANCH * D_MODEL)
N_IN = 9 * 512 + N_FOX + N_BRANCH * D_MODEL

kernel_name = "hybrid_diff_fox_stickbreak_macaron"


def _rmsnorm(x, g, eps=EPS):
    x32 = x.astype(jnp.float32)
    y = x32 * lax.rsqrt(jnp.mean(x32 * x32, axis=-1, keepdims=True) + eps)
    return (y * g.astype(jnp.float32)).astype(x.dtype)


def _swiglu(x, w1, w3, w2):
    return (jax.nn.silu(x @ w1) * (x @ w3)) @ w2


def _alibi_slopes(n_heads):
    return jnp.asarray([2.0 ** (-8.0 * (h + 1) / n_heads) for h in range(n_heads)], jnp.float32)


def _sweep(block_fn, seq):
    out = lax.map(block_fn, jnp.arange(seq // Q_BLOCK))
    nb, b, h, qb, e = out.shape
    return out.transpose(1, 0, 3, 2, 4).reshape(b, nb * qb, h * e)


def _diff_attention(q, k, v, lq1, lk1, lq2, lk2, subln_g, lam_init):
    b, s = q.shape[:2]
    q = q.reshape(b, s, N_DIFF, 2, HEAD_DIM)
    k = k.reshape(b, s, N_DIFF, 2, HEAD_DIM)
    v = v.reshape(b, s, N_DIFF, DIFF_V)
    f32 = jnp.float32
    lam = (jnp.exp(jnp.sum(lq1.astype(f32) * lk1.astype(f32)))
           - jnp.exp(jnp.sum(lq2.astype(f32) * lk2.astype(f32))) + lam_init)
    slopes = _alibi_slopes(N_DIFF)[:, None, None, None]
    scale = 1.0 / math.sqrt(HEAD_DIM)
    key_pos = jnp.arange(s)

    def block(i):
        qb = lax.dynamic_slice_in_dim(q, i * Q_BLOCK, Q_BLOCK, axis=1)
        dist = (i * Q_BLOCK + jnp.arange(Q_BLOCK))[:, None] - key_pos[None, :]
        logits = (jnp.einsum('bqhcd,bkhcd->bhcqk', qb, k).astype(f32) * scale
                  - slopes * dist.astype(f32))
        logits = jnp.where(dist >= 0, logits, -jnp.inf)
        p = jax.nn.softmax(logits, axis=-1)
        attn = p[:, :, 0] - lam * p[:, :, 1]
        return jnp.einsum('bhqk,bkhe->bhqe', attn.astype(v.dtype), v)

    o = _sweep(block, s).reshape(b, s, N_DIFF, DIFF_V)
    o = _rmsnorm(o, subln_g, SUBLN_EPS) * (1.0 - lam_init)
    return o.reshape(b, s, W_A)


def _forgetting_attention(q, k, v, f_logit, f_bias):
    b, s = q.shape[:2]
    q = q.reshape(b, s, N_FOX, HEAD_DIM)
    k = k.reshape(b, s, N_FOX, HEAD_DIM)
    v = v.reshape(b, s, N_FOX, HEAD_DIM)
    f32 = jnp.float32
    log_f = jax.nn.log_sigmoid(f_logit.astype(f32) + f_bias.astype(f32))
    c = jnp.cumsum(log_f, axis=1).transpose(0, 2, 1)
    scale = 1.0 / math.sqrt(HEAD_DIM)
    key_pos = jnp.arange(s)

    def block(i):
        qb = lax.dynamic_slice_in_dim(q, i * Q_BLOCK, Q_BLOCK, axis=1)
        cq = lax.dynamic_slice_in_dim(c, i * Q_BLOCK, Q_BLOCK, axis=2)
        dist = (i * Q_BLOCK + jnp.arange(Q_BLOCK))[:, None] - key_pos[None, :]
        logits = (jnp.einsum('bqhd,bkhd->bhqk', qb, k).astype(f32) * scale
                  + cq[..., :, None] - c[..., None, :])
        logits = jnp.where(dist >= 0, logits, -jnp.inf)
        p = jax.nn.softmax(logits, axis=-1)
        return jnp.einsum('bhqk,bkhe->bhqe', p.astype(v.dtype), v)

    return _sweep(block, s)


def _stick_breaking_attention(q, k, v):
    b, s = q.shape[:2]
    q = q.reshape(b, s, N_SB, HEAD_DIM)
    k = k.reshape(b, s, N_SB, HEAD_DIM)
    v = v.reshape(b, s, N_SB, HEAD_DIM)
    f32 = jnp.float32
    scale = 1.0 / math.sqrt(HEAD_DIM)
    key_pos = jnp.arange(s)

    def block(i):
        qb = lax.dynamic_slice_in_dim(q, i * Q_BLOCK, Q_BLOCK, axis=1)
        dist = (i * Q_BLOCK + jnp.arange(Q_BLOCK))[:, None] - key_pos[None, :]
        strict = dist > 0
        z = jnp.einsum('bqhd,bkhd->bhqk', qb, k).astype(f32) * scale
        log_keep = jnp.where(strict, jax.nn.log_sigmoid(-z), 0.0)
        later = lax.cumsum(log_keep, axis=3, reverse=True) - log_keep
        w = jnp.where(strict, jnp.exp(jax.nn.log_sigmoid(z) + later), 0.0)
        return jnp.einsum('bhqk,bkhe->bhqe', w.astype(v.dtype), v)

    return _sweep(block, s)


def setup_inputs(seed: int = 0) -> dict:
    key = jax.random.key(seed)
    ks = jax.random.split(key, 24)
    f32 = jnp.float32

    def nrm(k, shape, fan_in):
        return jax.random.normal(k, shape, f32) * fan_in ** -0.5

    def gain(k, shape):
        return 1.0 + 0.05 * jax.random.normal(k, shape, f32)

    return {
        "x": jax.random.normal(ks[0], (BATCH, SEQ, D_MODEL), f32),
        "ffn1_norm": gain(ks[1], (DEPTH, D_MODEL)),
        "ffn1_w1": nrm(ks[2], (DEPTH, D_MODEL, D_FF), D_MODEL),
        "ffn1_w3": nrm(ks[3], (DEPTH, D_MODEL, D_FF), D_MODEL),
        "ffn1_w2": nrm(ks[4], (DEPTH, D_FF, D_MODEL), D_FF),
        "mix_norm": gain(ks[5], (DEPTH, D_MODEL)),
        "w_in": nrm(ks[6], (DEPTH, D_MODEL, N_IN), D_MODEL),
        "forget_bias": jax.random.uniform(ks[7], (DEPTH, N_FOX), f32, 1.0, 4.0),
        "diff_lq1": 0.1 * jax.random.normal(ks[8], (DEPTH, HEAD_DIM), f32),
        "diff_lk1": 0.1 * jax.random.normal(ks[9], (DEPTH, HEAD_DIM), f32),
        "diff_lq2": 0.1 * jax.random.normal(ks[10], (DEPTH, HEAD_DIM), f32),
        "diff_lk2": 0.1 * jax.random.normal(ks[11], (DEPTH, HEAD_DIM), f32),
        "diff_subln": gain(ks[12], (DEPTH, DIFF_V)),
        "proj_a": nrm(ks[13], (DEPTH, W_A, D_MODEL), W_A),
        "proj_b": nrm(ks[14], (DEPTH, W_B, D_MODEL), W_B),
        "proj_c": nrm(ks[15], (DEPTH, W_C, D_MODEL), W_C),
        "w_out": nrm(ks[16], (DEPTH, D_MODEL, D_MODEL), D_MODEL),
        "ffn2_norm": gain(ks[17], (DEPTH, D_MODEL)),
        "ffn2_w1": nrm(ks[18], (DEPTH, D_MODEL, D_FF), D_MODEL),
        "ffn2_w3": nrm(ks[19], (DEPTH, D_MODEL, D_FF), D_MODEL),
        "ffn2_w2": nrm(ks[20], (DEPTH, D_FF, D_MODEL), D_FF),
        "final_norm": gain(ks[21], (D_MODEL,)),
    }


def reference(x, ffn1_norm, ffn1_w1, ffn1_w3, ffn1_w2, mix_norm, w_in, forget_bias,
              diff_lq1, diff_lk1, diff_lq2, diff_lk2, diff_subln, proj_a, proj_b, proj_c,
              w_out, ffn2_norm, ffn2_w1, ffn2_w3, ffn2_w2, final_norm):
    split_idx = np.cumsum(IN_SIZES)[:-1].tolist()
    h = x
    b, s, _ = x.shape
    for l in range(DEPTH):
        h = h + 0.5 * _swiglu(_rmsnorm(h, ffn1_norm[l]), ffn1_w1[l], ffn1_w3[l], ffn1_w2[l])

        u = _rmsnorm(h, mix_norm[l])
        z = u @ w_in[l]
        (a_q, a_k, a_v, b_q, b_k, b_v, b_f, c_q, c_k, c_v, g) = jnp.split(z, split_idx, axis=-1)
        lam_init = 0.8 - 0.6 * math.exp(-0.3 * l)
        o_a = _diff_attention(a_q, a_k, a_v, diff_lq1[l], diff_lk1[l], diff_lq2[l],
                              diff_lk2[l], diff_subln[l], lam_init)
        o_b = _forgetting_attention(b_q, b_k, b_v, b_f, forget_bias[l])
        o_c = _stick_breaking_attention(c_q, c_k, c_v)
        gates = jax.nn.sigmoid(g.reshape(b, s, N_BRANCH, D_MODEL))
        m = (gates[:, :, 0] * (o_a @ proj_a[l])
             + gates[:, :, 1] * (o_b @ proj_b[l])
             + gates[:, :, 2] * (o_c @ proj_c[l]))
        h = h + m @ w_out[l]

        h = h + 0.5 * _swiglu(_rmsnorm(h, ffn2_norm[l]), ffn2_w1[l], ffn2_w3[l], ffn2_w2[l])
    return _rmsnorm(h, final_norm)
```

```python
import functools
import math

import jax
import jax.numpy as jnp
from jax import lax
from jax.experimental import pallas as pl
from jax.experimental.pallas import tpu as pltpu

F32 = jnp.float32
BF16 = jnp.bfloat16

EPS = 1e-6
SUBLN_EPS = 1e-5
HEAD_DIM = 64
LANES = 128
N_GROUPS = 4
BRANCH_W = N_GROUPS * LANES
N_FOX = 8
QK_SCALE = 1.0 / math.sqrt(HEAD_DIM)
VMEM_LIMIT = 56 * 1024 * 1024

TM_DENSE = 512
TF_FFN = 256
TN_PROJ = 512
T_ATTN = 256
NEG_BIG = -1e30


def _rms(x, g, eps):
    return x * lax.rsqrt(jnp.mean(x * x, axis=-1, keepdims=True) + eps) * g


def _dot(a, b):
    return jnp.dot(a, b, preferred_element_type=F32)


def _dot_nt(a, b):
    return lax.dot_general(a, b, (((1,), (1,)), ((), ())), preferred_element_type=F32)


def _resident(shape):
    nd = len(shape)
    return pl.BlockSpec(shape, lambda *_: (0,) * nd, pipeline_mode=pl.Buffered(1))


def _params(n_grid):
    return pltpu.CompilerParams(dimension_semantics=("arbitrary",) * n_grid,
                                vmem_limit_bytes=VMEM_LIMIT)


def _ffn_kernel(h_ref, g_ref, w1_ref, w3_ref, w2_ref, fg_ref, o_ref, *, final):
    h = h_ref[...]
    nb = _rms(h, g_ref[...], EPS).astype(BF16)
    acc = jnp.zeros(h.shape, F32)
    for c in range(w1_ref.shape[1] // TF_FFN):
        sl = slice(c * TF_FFN, (c + 1) * TF_FFN)
        a = _dot(nb, w1_ref[:, sl])
        b = _dot(nb, w3_ref[:, sl])
        t = (a * jax.nn.sigmoid(a) * b).astype(BF16)
        acc = acc + _dot(t, w2_ref[sl, :])
    out = h + 0.5 * acc
    if final:
        out = _rms(out, fg_ref[...], EPS)
    o_ref[...] = out


def _ffn(h, g, w1, w3, w2, fg, final):
    t, d = h.shape
    ff = w1.shape[1]
    assert t % TM_DENSE == 0 and ff % TF_FFN == 0
    row = pl.BlockSpec((TM_DENSE, d), lambda i: (i, 0))
    return pl.pallas_call(
        functools.partial(_ffn_kernel, final=final),
        grid=(t // TM_DENSE,),
        in_specs=[row, _resident((1, d)), _resident((d, ff)), _resident((d, ff)),
                  _resident((ff, d)), _resident((1, d))],
        out_specs=row,
        out_shape=jax.ShapeDtypeStruct((t, d), F32),
        compiler_params=_params(1),
        name="ffn",
    )(h, g, w1, w3, w2, fg)


def _inproj_kernel(h_ref, g_ref, w_ref, wft_ref, z_ref, ft_ref):
    u = _rms(h_ref[...], g_ref[...], EPS).astype(BF16)
    for c in range(w_ref.shape[1] // TN_PROJ):
        sl = slice(c * TN_PROJ, (c + 1) * TN_PROJ)
        r = _dot(u, w_ref[:, sl])
        if c % 3 == 0:
            r = r * QK_SCALE
        z_ref[:, sl] = r.astype(BF16)
    ft_ref[...] = _dot_nt(wft_ref[...], u)


def _inproj(h, g, w_qkv, wft):
    t, d = h.shape
    n = w_qkv.shape[1]
    assert TN_PROJ == BRANCH_W and n == 9 * BRANCH_W
    nf = wft.shape[0]
    return pl.pallas_call(
        _inproj_kernel,
        grid=(t // TM_DENSE,),
        in_specs=[pl.BlockSpec((TM_DENSE, d), lambda i: (i, 0)), _resident((1, d)),
                  _resident((d, n)), _resident((nf, d))],
        out_specs=[pl.BlockSpec((TM_DENSE, n), lambda i: (i, 0)),
                   pl.BlockSpec((nf, TM_DENSE), lambda i: (0, i))],
        out_shape=[jax.ShapeDtypeStruct((t, n), BF16), jax.ShapeDtypeStruct((nf, t), F32)],
        compiler_params=_params(1),
        name="inproj",
    )(h, g, w_qkv, wft)


def _decay_kernel(ft_ref, bias_ref, c_ref):
    x = ft_ref[...] + bias_ref[...]
    x = jnp.minimum(x, 0.0) - jnp.log1p(jnp.exp(-jnp.abs(x)))
    s = x.shape[1]
    lane = lax.broadcasted_iota(jnp.int32, x.shape, 1)
    sh = 1
    while sh < s:
        x = x + jnp.where(lane >= sh, pltpu.roll(x, sh, axis=1), 0.0)
        sh *= 2
    c_ref[...] = x


def _decay(ft, bias, b, s):
    nf = ft.shape[0]
    return pl.pallas_call(
        _decay_kernel,
        grid=(b,),
        in_specs=[pl.BlockSpec((nf, s), lambda i: (0, i)), _resident((nf, 1))],
        out_specs=pl.BlockSpec((nf, s), lambda i: (0, i)),
        out_shape=jax.ShapeDtypeStruct((nf, b * s), F32),
        compiler_params=_params(1),
        name="decay",
    )(ft, bias)


def _lane_halves(q):
    lane = lax.broadcasted_iota(jnp.int32, q.shape, 1)
    zero = jnp.zeros_like(q)
    return jnp.where(lane < HEAD_DIM, q, zero), jnp.where(lane >= HEAD_DIM, q, zero)


def _softmax_step(s, v, m, l, acc):
    m_new = jnp.maximum(m, jnp.max(s, axis=-1, keepdims=True))
    p = jnp.exp(s - m_new)
    alpha = jnp.exp(m - m_new)
    l = alpha * l + jnp.sum(p, axis=-1, keepdims=True)
    acc = alpha * acc + _dot(p.astype(BF16), v)
    return m_new, l, acc


def _softmax_init(t):
    return (jnp.full((t, 1), NEG_BIG, F32), jnp.zeros((t, 1), F32), jnp.zeros((t, LANES), F32))


def _tile_mask(t, strict):
    row = lax.broadcasted_iota(jnp.int32, (t, t), 0)
    col = lax.broadcasted_iota(jnp.int32, (t, t), 1)
    return col < row if strict else col <= row


def _kv_tile(ref, j):
    return ref[0, pl.ds(pl.multiple_of(j * T_ATTN, T_ATTN), T_ATTN), :]


def _attn_specs(s, q_col, k_col, v_col):
    q_spec = pl.BlockSpec((1, T_ATTN, LANES), lambda b, g, i: (b, i, q_col + g))
    k_spec = pl.BlockSpec((1, s, LANES), lambda b, g, i: (b, 0, k_col + g))
    v_spec = pl.BlockSpec((1, s, LANES), lambda b, g, i: (b, 0, v_col + g))
    o_spec = pl.BlockSpec((1, T_ATTN, LANES), lambda b, g, i: (b, i, g))
    return q_spec, k_spec, v_spec, o_spec


def _diff_kernel(slopes_ref, q_ref, k_ref, v_ref, lam_ref, subln_ref, o_ref, *, lam_init):
    g = pl.program_id(1)
    i = pl.program_id(2)
    t = T_ATTN
    slope = slopes_ref[g]
    qs = _lane_halves(q_ref[0])
    kpos = lax.broadcasted_iota(jnp.int32, (1, t), 1).astype(F32)

    def tile(j, carry, mask):
        k = _kv_tile(k_ref, j)
        v = _kv_tile(v_ref, j)
        bias = slope * (kpos + ((j - i) * t).astype(F32))
        out = []
        for c in range(2):
            s = _dot_nt(qs[c], k) + bias
            if mask is not None:
                s = jnp.where(mask, s, -jnp.inf)
            out.extend(_softmax_step(s, v, *carry[3 * c:3 * c + 3]))
        return tuple(out)

    carry = lax.fori_loop(0, i, lambda j, c: tile(j, c, None), _softmax_init(t) + _softmax_init(t))
    m0, l0, a0, m1, l1, a1 = tile(i, carry, _tile_mask(t, strict=False))

    lv = lam_ref[...]
    lam = (jnp.exp(jnp.sum(lv[0:1] * lv[1:2], axis=-1, keepdims=True))
           - jnp.exp(jnp.sum(lv[2:3] * lv[3:4], axis=-1, keepdims=True)) + lam_init)
    o = a0 / l0 - lam * (a1 / l1)
    o = _rms(o, subln_ref[...], SUBLN_EPS) * (1.0 - lam_init)
    o_ref[0] = o.astype(o_ref.dtype)


def _diff_attn(z, slopes, lam_rows, subln, lam_init):
    b, s, _ = z.shape
    q_spec, k_spec, v_spec, o_spec = _attn_specs(s, 0, N_GROUPS, 2 * N_GROUPS)
    return pl.pallas_call(
        functools.partial(_diff_kernel, lam_init=lam_init),
        grid=(b, N_GROUPS, s // T_ATTN),
        in_specs=[pl.BlockSpec(memory_space=pltpu.SMEM), q_spec, k_spec, v_spec,
                  _resident(lam_rows.shape), _resident(subln.shape)],
        out_specs=o_spec,
        out_shape=jax.ShapeDtypeStruct((b, s, BRANCH_W), BF16),
        compiler_params=_params(3),
        name="diff_attn",
    )(slopes, z, z, z, lam_rows, subln)


def _fox_kernel(q_ref, k_ref, v_ref, c_ref, o_ref):
    i = pl.program_id(2)
    t = T_ATTN
    qs = _lane_halves(q_ref[0])

    def tile(j, carry, mask):
        k = _kv_tile(k_ref, j)
        v = _kv_tile(v_ref, j)
        out = []
        for h in range(2):
            c_row = c_ref[0, h:h + 1, pl.ds(pl.multiple_of(j * t, t), t)]
            s = _dot_nt(qs[h], k) - c_row
            if mask is not None:
                s = jnp.where(mask, s, -jnp.inf)
            out.extend(_softmax_step(s, v, *carry[3 * h:3 * h + 3]))
        return tuple(out)

    carry = lax.fori_loop(0, i, lambda j, c: tile(j, c, None), _softmax_init(t) + _softmax_init(t))
    m0, l0, a0, m1, l1, a1 = tile(i, carry, _tile_mask(t, strict=False))
    lane = lax.broadcasted_iota(jnp.int32, a0.shape, 1)
    o_ref[0] = jnp.where(lane < HEAD_DIM, a0 / l0, a1 / l1).astype(o_ref.dtype)


def _fox_attn(z, c):
    b, s, _ = z.shape
    q_spec, k_spec, v_spec, o_spec = _attn_specs(s, 3 * N_GROUPS, 4 * N_GROUPS, 5 * N_GROUPS)
    c_spec = pl.BlockSpec((1, 2, s), lambda b_, g, i: (g, 0, b_))
    return pl.pallas_call(
        _fox_kernel,
        grid=(b, N_GROUPS, s // T_ATTN),
        in_specs=[q_spec, k_spec, v_spec, c_spec],
        out_specs=o_spec,
        out_shape=jax.ShapeDtypeStruct((b, s, BRANCH_W), BF16),
        compiler_params=_params(3),
        name="fox_attn",
    )(z, z, z, c)


def _sb_kernel(q_ref, k_ref, v_ref, tri_ref, o_ref):
    i = pl.program_id(2)
    t = T_ATTN
    qs = _lane_halves(q_ref[0])
    tri = tri_ref[...]

    def tile(j, carry, mask):
        k = _kv_tile(k_ref, j)
        v = _kv_tile(v_ref, j)
        out = []
        for h in range(2):
            run, acc = carry[2 * h:2 * h + 2]
            z = _dot_nt(qs[h], k)
            log_beta = jnp.minimum(z, 0.0) - jnp.log1p(jnp.exp(-jnp.abs(z)))
            log_keep = log_beta - z
            if mask is not None:
                log_keep = jnp.where(mask, log_keep, 0.0)
            hi = log_keep.astype(BF16)
            lo = (log_keep - hi.astype(F32)).astype(BF16)
            later = _dot(hi, tri) + _dot(lo, tri)
            w = jnp.exp(log_beta + later + run)
            if mask is not None:
                w = jnp.where(mask, w, 0.0)
            acc = acc + _dot(w.astype(BF16), v)
            run = run + jnp.sum(log_keep, axis=-1, keepdims=True)
            out.extend((run, acc))
        return tuple(out)

    init = (jnp.zeros((t, 1), F32), jnp.zeros((t, LANES), F32)) * 2
    carry = tile(i, init, _tile_mask(t, strict=True))
    _, a0, _, a1 = lax.fori_loop(0, i, lambda n, c: tile(i - 1 - n, c, None), carry)
    lane = lax.broadcasted_iota(jnp.int32, a0.shape, 1)
    o_ref[0] = jnp.where(lane < HEAD_DIM, a0, a1).astype(o_ref.dtype)


def _sb_attn(z, tri):
    b, s, _ = z.shape
    q_spec, k_spec, v_spec, o_spec = _attn_specs(s, 6 * N_GROUPS, 7 * N_GROUPS, 8 * N_GROUPS)
    return pl.pallas_call(
        _sb_kernel,
        grid=(b, N_GROUPS, s // T_ATTN),
        in_specs=[q_spec, k_spec, v_spec, _resident(tri.shape)],
        out_specs=o_spec,
        out_shape=jax.ShapeDtypeStruct((b, s, BRANCH_W), BF16),
        compiler_params=_params(3),
        name="sb_attn",
    )(z, z, z, tri)


def _mixout_kernel(h_ref, g_ref, oa_ref, ob_ref, oc_ref, wg_ref, pa_ref, pb_ref, pc_ref, wo_ref,
                   o_ref):
    h = h_ref[...]
    d = h.shape[1]
    u = _rms(h, g_ref[...], EPS).astype(BF16)
    m = jnp.zeros(h.shape, F32)
    for n, (o_r, p_r) in enumerate(((oa_ref, pa_ref), (ob_ref, pb_ref), (oc_ref, pc_ref))):
        gate = jax.nn.sigmoid(_dot(u, wg_ref[:, n * d:(n + 1) * d]))
        m = m + gate * _dot(o_r[...], p_r[...])
    o_ref[...] = h + _dot(m.astype(BF16), wo_ref[...])


def _mixout(h, g, oa, ob, oc, wg, pa, pb, pc, wo):
    t, d = h.shape
    row = pl.BlockSpec((TM_DENSE, d), lambda i: (i, 0))
    orow = pl.BlockSpec((TM_DENSE, BRANCH_W), lambda i: (i, 0))
    return pl.pallas_call(
        _mixout_kernel,
        grid=(t // TM_DENSE,),
        in_specs=[row, _resident((1, d)), orow, orow, orow, _resident(wg.shape),
                  _resident(pa.shape), _resident(pb.shape), _resident(pc.shape),
                  _resident(wo.shape)],
        out_specs=row,
        out_shape=jax.ShapeDtypeStruct((t, d), F32),
        compiler_params=_params(1),
        name="mixout",
    )(h, g, oa, ob, oc, wg, pa, pb, pc, wo)


def kernel(x, ffn1_norm, ffn1_w1, ffn1_w3, ffn1_w2, mix_norm, w_in, forget_bias, diff_lq1, diff_lk1,
           diff_lq2, diff_lk2, diff_subln, proj_a, proj_b, proj_c, w_out, ffn2_norm, ffn2_w1,
           ffn2_w3, ffn2_w2, final_norm):
    b, s, d = x.shape
    depth = w_in.shape[0]
    assert s % T_ATTN == 0 and (b * s) % TM_DENSE == 0
    n_qkv_a = 6 * BRANCH_W
    f_lo, f_hi = n_qkv_a, n_qkv_a + N_FOX
    g_lo = f_hi + 3 * BRANCH_W

    slopes = jnp.asarray([2.0 ** (-8.0 * (hh + 1) / N_GROUPS) for hh in range(N_GROUPS)], F32)
    ridx = lax.broadcasted_iota(jnp.int32, (T_ATTN, T_ATTN), 0)
    cidx = lax.broadcasted_iota(jnp.int32, (T_ATTN, T_ATTN), 1)
    tri = (ridx > cidx).astype(BF16)
    row = lambda a: a.reshape(1, -1).astype(F32)
    pad_rows = 16

    h = x.reshape(b * s, d)
    for l in range(depth):
        bf = lambda a: a[l].astype(BF16)
        h = _ffn(h, row(ffn1_norm[l]), bf(ffn1_w1), bf(ffn1_w3), bf(ffn1_w2), row(final_norm),
                 final=False)

        w = w_in[l]
        w_qkv = jnp.concatenate([w[:, :f_lo], w[:, f_hi:g_lo]], axis=1).astype(BF16)
        wft = jnp.zeros((pad_rows, d), BF16).at[:N_FOX].set(w[:, f_lo:f_hi].T.astype(BF16))
        z, ft = _inproj(h, row(mix_norm[l]), w_qkv, wft)
        z = z.reshape(b, s, -1)

        bias = jnp.zeros((pad_rows, 1), F32).at[:N_FOX, 0].set(forget_bias[l])
        c = _decay(ft, bias, b, s)[:N_FOX].reshape(N_GROUPS, 2, b * s)

        lam_init = 0.8 - 0.6 * math.exp(-0.3 * l)
        lam_rows = jnp.zeros((8, LANES), F32).at[:4, :HEAD_DIM].set(
            jnp.stack([diff_lq1[l], diff_lk1[l], diff_lq2[l], diff_lk2[l]]))
        o_a = _diff_attn(z, slopes, lam_rows, row(diff_subln[l]), lam_init)
        o_b = _fox_attn(z, c)
        o_c = _sb_attn(z, tri)

        flat = lambda a: a.reshape(b * s, BRANCH_W)
        h = _mixout(h, row(mix_norm[l]), flat(o_a), flat(o_b), flat(o_c), w[:, g_lo:].astype(BF16),
                    bf(proj_a), bf(proj_b), bf(proj_c), bf(w_out))

        h = _ffn(h, row(ffn2_norm[l]), bf(ffn2_w1), bf(ffn2_w3), bf(ffn2_w2), row(final_norm),
                 final=(l == depth - 1))
    return h.reshape(b, s, d)
```

```python
import functools
import math

import jax
import jax.numpy as jnp
from jax import lax
from jax.experimental import pallas as pl
from jax.experimental.pallas import tpu as pltpu

F32 = jnp.float32
BF16 = jnp.bfloat16

EPS = 1e-6
SUBLN_EPS = 1e-5
HEAD_DIM = 64
LANES = 128
BF16_ROWS = 16
N_GROUPS = 4
N_CHAINS = 2 * N_GROUPS
BRANCH_W = N_GROUPS * LANES
N_FOX = 8
N_SPLIT = 3
QK_SCALE = 1.0 / math.sqrt(HEAD_DIM)
POS_RADIX = 256
VMEM_LIMIT = 56 * 1024 * 1024

TM_DENSE = 512
TF_FFN = 256
TN_PROJ = 512
T_ATTN = 256
NEG_BIG = -1e30


def _rms(x, g, eps):
    return x * lax.rsqrt(jnp.mean(x * x, axis=-1, keepdims=True) + eps) * g


def _dot(a, b):
    return jnp.dot(a, b, preferred_element_type=F32)


def _dot_nt(a, b):
    return lax.dot_general(a, b, (((1,), (1,)), ((), ())), preferred_element_type=F32)


def _resident(shape):
    nd = len(shape)
    return pl.BlockSpec(shape, lambda *_: (0,) * nd, pipeline_mode=pl.Buffered(1))


def _params(n_grid):
    return pltpu.CompilerParams(dimension_semantics=("arbitrary",) * n_grid,
                                vmem_limit_bytes=VMEM_LIMIT)


def _split3(x):
    hi = x.astype(BF16).astype(F32)
    r = x - hi
    mid = r.astype(BF16).astype(F32)
    return hi, mid, r - mid


def _ffn_kernel(h_ref, g_ref, w1_ref, w3_ref, w2_ref, fg_ref, o_ref, *, final):
    h = h_ref[...]
    nb = _rms(h, g_ref[...], EPS).astype(BF16)
    acc = jnp.zeros(h.shape, F32)
    for c in range(w1_ref.shape[1] // TF_FFN):
        sl = slice(c * TF_FFN, (c + 1) * TF_FFN)
        a = _dot(nb, w1_ref[:, sl])
        b = _dot(nb, w3_ref[:, sl])
        t = (a * jax.nn.sigmoid(a) * b).astype(BF16)
        acc = acc + _dot(t, w2_ref[sl, :])
    out = h + 0.5 * acc
    if final:
        out = _rms(out, fg_ref[...], EPS)
    o_ref[...] = out


def _ffn(h, g, w1, w3, w2, fg, final):
    t, d = h.shape
    ff = w1.shape[1]
    assert t % TM_DENSE == 0 and ff % TF_FFN == 0
    row = pl.BlockSpec((TM_DENSE, d), lambda i: (i, 0))
    return pl.pallas_call(
        functools.partial(_ffn_kernel, final=final),
        grid=(t // TM_DENSE,),
        in_specs=[row, _resident((1, d)), _resident((d, ff)), _resident((d, ff)),
                  _resident((ff, d)), _resident((1, d))],
        out_specs=row,
        out_shape=jax.ShapeDtypeStruct((t, d), F32),
        compiler_params=_params(1),
        name="ffn",
    )(h, g, w1, w3, w2, fg)


def _inproj_kernel(h_ref, g_ref, wk_ref, wqvt_ref, wft_ref, k_ref, qvt_ref, ft_ref):
    u = _rms(h_ref[...], g_ref[...], EPS).astype(BF16)
    for c in range(wk_ref.shape[1] // TN_PROJ):
        sl = slice(c * TN_PROJ, (c + 1) * TN_PROJ)
        k_ref[:, sl] = _dot(u, wk_ref[:, sl]).astype(BF16)
    n_chunks = wqvt_ref.shape[0] // TN_PROJ
    for c in range(n_chunks):
        sl = slice(c * TN_PROJ, (c + 1) * TN_PROJ)
        r = _dot_nt(wqvt_ref[sl, :], u)
        if c < n_chunks // 2:
            r = r * QK_SCALE
        qvt_ref[sl, :] = r.astype(BF16)
    ft_ref[...] = _dot_nt(wft_ref[...], u)


def _inproj(h, g, w_k, w_qvt, w_ft):
    t, d = h.shape
    nk, nqv, nf = w_k.shape[1], w_qvt.shape[0], w_ft.shape[0]
    assert TN_PROJ == BRANCH_W and nk == 3 * BRANCH_W and nqv == 6 * BRANCH_W
    return pl.pallas_call(
        _inproj_kernel,
        grid=(t // TM_DENSE,),
        in_specs=[pl.BlockSpec((TM_DENSE, d), lambda i: (i, 0)), _resident((1, d)),
                  _resident((d, nk)), _resident((nqv, d)), _resident((nf, d))],
        out_specs=[pl.BlockSpec((TM_DENSE, nk), lambda i: (i, 0)),
                   pl.BlockSpec((nqv, TM_DENSE), lambda i: (0, i)),
                   pl.BlockSpec((nf, TM_DENSE), lambda i: (0, i))],
        out_shape=[jax.ShapeDtypeStruct((t, nk), BF16), jax.ShapeDtypeStruct((nqv, t), BF16),
                   jax.ShapeDtypeStruct((nf, t), F32)],
        compiler_params=_params(1),
        name="inproj",
    )(h, g, w_k, w_qvt, w_ft)


def _decay_kernel(ft_ref, bias_ref, sel_ref, kc_ref):
    x = ft_ref[...] + bias_ref[...]
    x = jnp.minimum(x, 0.0) - jnp.log1p(jnp.exp(-jnp.abs(x)))
    nf, s = x.shape
    lane = lax.broadcasted_iota(jnp.int32, x.shape, 1)
    sh = 1
    while sh < s:
        x = x + jnp.where(lane >= sh, pltpu.roll(x, sh, axis=1), 0.0)
        sh *= 2
    parts = jnp.concatenate(_split3(x) + (jnp.zeros((LANES - N_SPLIT * nf, s), F32),), axis=0)
    kc_ref[0] = _dot(parts.T.astype(BF16), sel_ref[...]).astype(BF16)


def _decay(ft, bias, sel, b, s):
    nf = ft.shape[0]
    return pl.pallas_call(
        _decay_kernel,
        grid=(b,),
        in_specs=[pl.BlockSpec((nf, s), lambda i: (0, i)), _resident((nf, 1)),
                  _resident(sel.shape)],
        out_specs=pl.BlockSpec((1, s, BRANCH_W), lambda i: (i, 0, 0)),
        out_shape=jax.ShapeDtypeStruct((b, s, BRANCH_W), BF16),
        compiler_params=_params(1),
        name="decay",
    )(ft, bias, sel)


def _feat_iota(t):
    return lax.broadcasted_iota(jnp.int32, (LANES, t), 0)


def _group(n):
    return slice((n // 2) * LANES, (n // 2 + 1) * LANES)


def _chain_queries(qt_ref, n, feat=None):
    qt = qt_ref[_group(n), :]
    sub = _feat_iota(qt.shape[1])
    lo = (n % 2) * HEAD_DIM
    q = jnp.where((sub >= lo) & (sub < lo + HEAD_DIM), qt, jnp.zeros_like(qt))
    return q if feat is None else jnp.concatenate([q, feat], axis=0)


def _key_rows(ref, j, n):
    return ref[0, pl.ds(pl.multiple_of(j * T_ATTN, T_ATTN), T_ATTN), _group(n)]


def _value_cols(ref, j, n):
    return ref[_group(n), pl.ds(pl.multiple_of(j * T_ATTN, T_ATTN), T_ATTN)]


def _softmax_step(s, vt, m, l, acc):
    m_new = jnp.maximum(m, jnp.max(s, axis=0, keepdims=True))
    p = jnp.exp(s - m_new)
    alpha = jnp.exp(m - m_new)
    l = alpha * l + jnp.sum(p, axis=0, keepdims=True)
    acc = alpha * acc + _dot(vt, p.astype(BF16))
    return m_new, l, acc


def _softmax_tiles(i, scores, values):
    t = T_ATTN
    key = lax.broadcasted_iota(jnp.int32, (t, t), 0)
    qry = lax.broadcasted_iota(jnp.int32, (t, t), 1)

    def tile(j, stats, diagonal):
        ss = [scores(j, n) for n in range(N_CHAINS)]
        out = []
        for n in range(N_CHAINS):
            s = jnp.where(key <= qry, ss[n], -jnp.inf) if diagonal else ss[n]
            out.extend(_softmax_step(s, values(j, n), *stats[3 * n:3 * n + 3]))
        return tuple(out)

    init = (jnp.full((1, t), NEG_BIG, F32), jnp.zeros((1, t), F32),
            jnp.zeros((LANES, t), F32)) * N_CHAINS
    stats = lax.fori_loop(0, i, lambda j, c: tile(j, c, False), init)
    stats = tile(i, stats, True)
    return [stats[3 * n + 2] * (1.0 / stats[3 * n + 1]) for n in range(N_CHAINS)]


def _head_rows(a0, a1):
    sub = lax.broadcasted_iota(jnp.int32, a0.shape, 0)
    return jnp.where(sub < HEAD_DIM, a0, a1)


def _attn_specs(s, branch):
    nq = s // T_ATTN
    qt_spec = pl.BlockSpec((BRANCH_W, T_ATTN), lambda b, i: (branch, b * nq + i))
    k_spec = pl.BlockSpec((1, s, BRANCH_W), lambda b, i: (b, 0, branch))
    vt_spec = pl.BlockSpec((BRANCH_W, s), lambda b, i: (3 + branch, b))
    o_spec = pl.BlockSpec((1, T_ATTN, BRANCH_W), lambda b, i: (b, i, 0))
    return qt_spec, k_spec, vt_spec, o_spec


def _diff_kernel(slopes_ref, qt_ref, k_ref, pos_ref, vt_ref, lam_ref, subln_ref, o_ref, *, lam_init):
    i = pl.program_id(1)
    t = T_ATTN
    sub = _feat_iota(t)
    qs = []
    for n in range(N_CHAINS):
        slope = slopes_ref[n // 2]
        feat = jnp.where(sub == 0, slope * POS_RADIX, jnp.where(sub == 1, slope, 0.0)).astype(BF16)
        qs.append(_chain_queries(qt_ref, n, feat))

    def scores(j, n):
        k = jnp.concatenate([_key_rows(k_ref, j, n), _key_rows(pos_ref, j, 0)], axis=1)
        return _dot(k, qs[n])

    acc = _softmax_tiles(i, scores, lambda j, n: _value_cols(vt_ref, j, n))

    lv = lam_ref[...]
    lam = (jnp.exp(jnp.sum(lv[0:1] * lv[1:2], axis=-1, keepdims=True))
           - jnp.exp(jnp.sum(lv[2:3] * lv[3:4], axis=-1, keepdims=True)) + lam_init)
    for g in range(N_GROUPS):
        o = (acc[2 * g] - lam * acc[2 * g + 1]).T
        o = _rms(o, subln_ref[...], SUBLN_EPS) * (1.0 - lam_init)
        o_ref[0, :, g * LANES:(g + 1) * LANES] = o.astype(o_ref.dtype)


def _diff_attn(k, qvt, pos, slopes, lam_rows, subln, lam_init):
    b, s, _ = k.shape
    qt_spec, k_spec, vt_spec, o_spec = _attn_specs(s, 0)
    return pl.pallas_call(
        functools.partial(_diff_kernel, lam_init=lam_init),
        grid=(b, s // T_ATTN),
        in_specs=[pl.BlockSpec(memory_space=pltpu.SMEM), qt_spec, k_spec, _resident(pos.shape),
                  vt_spec, _resident(lam_rows.shape), _resident(subln.shape)],
        out_specs=o_spec,
        out_shape=jax.ShapeDtypeStruct((b, s, BRANCH_W), BF16),
        compiler_params=_params(2),
        name="diff_attn",
    )(slopes, qvt, k, pos, qvt, lam_rows, subln)


def _fox_kernel(qt_ref, k_ref, kc_ref, vt_ref, o_ref):
    i = pl.program_id(1)
    t = T_ATTN
    sub = _feat_iota(t)
    qs = []
    for n in range(N_CHAINS):
        lo = N_SPLIT * (n % 2)
        feat = jnp.where((sub >= lo) & (sub < lo + N_SPLIT), -1.0, 0.0).astype(BF16)
        qs.append(_chain_queries(qt_ref, n, feat))

    def scores(j, n):
        k = jnp.concatenate([_key_rows(k_ref, j, n), _key_rows(kc_ref, j, n)], axis=1)
        return _dot(k, qs[n])

    acc = _softmax_tiles(i, scores, lambda j, n: _value_cols(vt_ref, j, n))
    for g in range(N_GROUPS):
        o_ref[0, :, g * LANES:(g + 1) * LANES] = _head_rows(
            acc[2 * g], acc[2 * g + 1]).T.astype(o_ref.dtype)


def _fox_attn(k, qvt, kc):
    b, s, _ = k.shape
    qt_spec, k_spec, vt_spec, o_spec = _attn_specs(s, 1)
    kc_spec = pl.BlockSpec((1, s, BRANCH_W), lambda b_, i: (b_, 0, 0))
    return pl.pallas_call(
        _fox_kernel,
        grid=(b, s // T_ATTN),
        in_specs=[qt_spec, k_spec, kc_spec, vt_spec],
        out_specs=o_spec,
        out_shape=jax.ShapeDtypeStruct((b, s, BRANCH_W), BF16),
        compiler_params=_params(2),
        name="fox_attn",
    )(qvt, k, kc, qvt)


def _sb_kernel(qt_ref, k_ref, vt_ref, tri_ref, o_ref):
    i = pl.program_id(1)
    t = T_ATTN
    qs = [_chain_queries(qt_ref, n) for n in range(N_CHAINS)]
    tri = tri_ref[...]
    key = lax.broadcasted_iota(jnp.int32, (t, t), 0)
    qry = lax.broadcasted_iota(jnp.int32, (t, t), 1)

    def tile(j, carry, diagonal):
        zs = [_dot(_key_rows(k_ref, j, n), qs[n]) for n in range(N_CHAINS)]
        mid = []
        for n in range(N_CHAINS):
            z = zs[n]
            log_beta = jnp.minimum(z, 0.0) - jnp.log(1.0 + jnp.exp(jnp.minimum(z, -z)))
            log_keep = log_beta - z
            if diagonal:
                log_keep = jnp.where(key < qry, log_keep, 0.0)
            hi = log_keep.astype(BF16)
            lo = (log_keep - hi.astype(F32)).astype(BF16)
            later = _dot(tri, hi) + _dot(tri, lo)
            mid.append((log_beta + later, jnp.sum(log_keep, axis=0, keepdims=True)))
        out = []
        for n in range(N_CHAINS):
            run, acc = carry[2 * n:2 * n + 2]
            w = jnp.exp(mid[n][0] + run)
            if diagonal:
                w = jnp.where(key < qry, w, 0.0)
            acc = acc + _dot(_value_cols(vt_ref, j, n), w.astype(BF16))
            out.extend((run + mid[n][1], acc))
        return tuple(out)

    init = (jnp.zeros((1, t), F32), jnp.zeros((LANES, t), F32)) * N_CHAINS
    carry = tile(i, init, True)
    carry = lax.fori_loop(0, i, lambda n, c: tile(i - 1 - n, c, False), carry)
    for g in range(N_GROUPS):
        o_ref[0, :, g * LANES:(g + 1) * LANES] = _head_rows(
            carry[4 * g + 1], carry[4 * g + 3]).T.astype(o_ref.dtype)


def _sb_attn(k, qvt, tri):
    b, s, _ = k.shape
    qt_spec, k_spec, vt_spec, o_spec = _attn_specs(s, 2)
    return pl.pallas_call(
        _sb_kernel,
        grid=(b, s // T_ATTN),
        in_specs=[qt_spec, k_spec, vt_spec, _resident(tri.shape)],
        out_specs=o_spec,
        out_shape=jax.ShapeDtypeStruct((b, s, BRANCH_W), BF16),
        compiler_params=_params(2),
        name="sb_attn",
    )(qvt, k, qvt, tri)


def _mixout_kernel(h_ref, g_ref, oa_ref, ob_ref, oc_ref, wg_ref, pa_ref, pb_ref, pc_ref, wo_ref,
                   o_ref):
    h = h_ref[...]
    d = h.shape[1]
    u = _rms(h, g_ref[...], EPS).astype(BF16)
    m = jnp.zeros(h.shape, F32)
    for n, (o_r, p_r) in enumerate(((oa_ref, pa_ref), (ob_ref, pb_ref), (oc_ref, pc_ref))):
        gate = jax.nn.sigmoid(_dot(u, wg_ref[:, n * d:(n + 1) * d]))
        m = m + gate * _dot(o_r[...], p_r[...])
    o_ref[...] = h + _dot(m.astype(BF16), wo_ref[...])


def _mixout(h, g, oa, ob, oc, wg, pa, pb, pc, wo):
    t, d = h.shape
    row = pl.BlockSpec((TM_DENSE, d), lambda i: (i, 0))
    orow = pl.BlockSpec((TM_DENSE, BRANCH_W), lambda i: (i, 0))
    return pl.pallas_call(
        _mixout_kernel,
        grid=(t // TM_DENSE,),
        in_specs=[row, _resident((1, d)), orow, orow, orow, _resident(wg.shape),
                  _resident(pa.shape), _resident(pb.shape), _resident(pc.shape),
                  _resident(wo.shape)],
        out_specs=row,
        out_shape=jax.ShapeDtypeStruct((t, d), F32),
        compiler_params=_params(1),
        name="mixout",
    )(h, g, oa, ob, oc, wg, pa, pb, pc, wo)


def _decay_selector():
    sel = [[0.0] * BRANCH_W for _ in range(LANES)]
    for head in range(N_FOX):
        for piece in range(N_SPLIT):
            sel[piece * BF16_ROWS + head][(head // 2) * LANES + N_SPLIT * (head % 2) + piece] = 1.0
    return jnp.asarray(sel, BF16)


def kernel(x, ffn1_norm, ffn1_w1, ffn1_w3, ffn1_w2, mix_norm, w_in, forget_bias, diff_lq1, diff_lk1,
           diff_lq2, diff_lk2, diff_subln, proj_a, proj_b, proj_c, w_out, ffn2_norm, ffn2_w1,
           ffn2_w3, ffn2_w2, final_norm):
    b, s, d = x.shape
    depth = w_in.shape[0]
    assert s % T_ATTN == 0 and (b * s) % TM_DENSE == 0 and s <= POS_RADIX * POS_RADIX
    bw = BRANCH_W
    f_lo = 6 * bw
    c_lo = f_lo + N_FOX
    g_lo = c_lo + 3 * bw

    slopes = jnp.asarray([2.0 ** (-8.0 * (hh + 1) / N_GROUPS) for hh in range(N_GROUPS)], F32)
    ridx = lax.broadcasted_iota(jnp.int32, (T_ATTN, T_ATTN), 0)
    cidx = lax.broadcasted_iota(jnp.int32, (T_ATTN, T_ATTN), 1)
    tri = (cidx > ridx).astype(BF16)
    kpos = lax.broadcasted_iota(jnp.int32, (s, LANES), 0)
    plane = lax.broadcasted_iota(jnp.int32, (s, LANES), 1)
    pos = jnp.where(plane == 0, kpos // POS_RADIX,
                    jnp.where(plane == 1, kpos % POS_RADIX, 0)).astype(BF16)[None]
    sel = _decay_selector()
    row = lambda a: a.reshape(1, -1).astype(F32)

    h = x.reshape(b * s, d)
    for l in range(depth):
        bf = lambda a: a[l].astype(BF16)
        h = _ffn(h, row(ffn1_norm[l]), bf(ffn1_w1), bf(ffn1_w3), bf(ffn1_w2), row(final_norm),
                 final=False)

        w = w_in[l]
        col = lambda lo: w[:, lo:lo + bw]
        w_k = jnp.concatenate([col(bw), col(4 * bw), col(c_lo + bw)], axis=1).astype(BF16)
        w_qvt = jnp.concatenate([col(0), col(3 * bw), col(c_lo),
                                 col(2 * bw), col(5 * bw), col(c_lo + 2 * bw)], axis=1).T.astype(BF16)
        w_ft = jnp.zeros((BF16_ROWS, d), BF16).at[:N_FOX].set(w[:, f_lo:c_lo].T.astype(BF16))
        k, qvt, ft = _inproj(h, row(mix_norm[l]), w_k, w_qvt, w_ft)
        k = k.reshape(b, s, -1)

        bias = jnp.zeros((BF16_ROWS, 1), F32).at[:N_FOX, 0].set(forget_bias[l])
        kc = _decay(ft, bias, sel, b, s)

        lam_init = 0.8 - 0.6 * math.exp(-0.3 * l)
        lam_rows = jnp.zeros((8, LANES), F32).at[:4, :HEAD_DIM].set(
            jnp.stack([diff_lq1[l], diff_lk1[l], diff_lq2[l], diff_lk2[l]]))
        o_a = _diff_attn(k, qvt, pos, slopes, lam_rows, row(diff_subln[l]), lam_init)
        o_b = _fox_attn(k, qvt, kc)
        o_c = _sb_attn(k, qvt, tri)

        flat = lambda a: a.reshape(b * s, bw)
        h = _mixout(h, row(mix_norm[l]), flat(o_a), flat(o_b), flat(o_c), w[:, g_lo:].astype(BF16),
                    bf(proj_a), bf(proj_b), bf(proj_c), bf(w_out))

        h = _ffn(h, row(ffn2_norm[l]), bf(ffn2_w1), bf(ffn2_w3), bf(ffn2_w2), row(final_norm),
                 final=(l == depth - 1))
    return h.reshape(b, s, d)
```

```python
import functools
import math

import jax
import jax.numpy as jnp
from jax import lax
from jax.experimental import pallas as pl
from jax.experimental.pallas import tpu as pltpu

F32 = jnp.float32
BF16 = jnp.bfloat16

EPS = 1e-6
SUBLN_EPS = 1e-5
HEAD_DIM = 64
LANES = 128
BF16_ROWS = 16
N_GROUPS = 4
N_CHAINS = 2 * N_GROUPS
BRANCH_W = N_GROUPS * LANES
N_FOX = 8
N_SPLIT = 3
QK_SCALE = 1.0 / math.sqrt(HEAD_DIM)
POS_RADIX = 256
VMEM_LIMIT = 56 * 1024 * 1024

TM_DENSE = 512
TF_FFN = 256
TN_PROJ = 512
T_ATTN = 256
NEG_BIG = -1e30


def _rms(x, g, eps):
    return x * lax.rsqrt(jnp.mean(x * x, axis=-1, keepdims=True) + eps) * g


def _dot(a, b):
    return jnp.dot(a, b, preferred_element_type=F32)


def _dot_nt(a, b):
    return lax.dot_general(a, b, (((1,), (1,)), ((), ())), preferred_element_type=F32)


def _resident(shape):
    nd = len(shape)
    return pl.BlockSpec(shape, lambda *_: (0,) * nd, pipeline_mode=pl.Buffered(1))


def _params(n_grid):
    return pltpu.CompilerParams(dimension_semantics=("arbitrary",) * n_grid,
                                vmem_limit_bytes=VMEM_LIMIT)


def _split3(x):
    hi = x.astype(BF16).astype(F32)
    r = x - hi
    mid = r.astype(BF16).astype(F32)
    return hi, mid, r - mid


def _ffn_kernel(h_ref, g_ref, w1_ref, w3_ref, w2_ref, fg_ref, o_ref, *, final):
    h = h_ref[...]
    nb = _rms(h, g_ref[...], EPS).astype(BF16)
    acc = jnp.zeros(h.shape, F32)
    for c in range(w1_ref.shape[1] // TF_FFN):
        sl = slice(c * TF_FFN, (c + 1) * TF_FFN)
        a = _dot(nb, w1_ref[:, sl])
        b = _dot(nb, w3_ref[:, sl])
        t = (a * jax.nn.sigmoid(a) * b).astype(BF16)
        acc = acc + _dot(t, w2_ref[sl, :])
    out = h + 0.5 * acc
    if final:
        out = _rms(out, fg_ref[...], EPS)
    o_ref[...] = out


def _ffn(h, g, w1, w3, w2, fg, final):
    t, d = h.shape
    ff = w1.shape[1]
    assert t % TM_DENSE == 0 and ff % TF_FFN == 0
    row = pl.BlockSpec((TM_DENSE, d), lambda i: (i, 0))
    return pl.pallas_call(
        functools.partial(_ffn_kernel, final=final),
        grid=(t // TM_DENSE,),
        in_specs=[row, _resident((1, d)), _resident((d, ff)), _resident((d, ff)),
                  _resident((ff, d)), _resident((1, d))],
        out_specs=row,
        out_shape=jax.ShapeDtypeStruct((t, d), F32),
        compiler_params=_params(1),
        name="ffn",
    )(h, g, w1, w3, w2, fg)


def _inproj_kernel(h_ref, g_ref, wk_ref, wqvt_ref, wft_ref, k_ref, qvt_ref, ft_ref):
    u = _rms(h_ref[...], g_ref[...], EPS).astype(BF16)
    for c in range(wk_ref.shape[1] // TN_PROJ):
        sl = slice(c * TN_PROJ, (c + 1) * TN_PROJ)
        k_ref[:, sl] = _dot(u, wk_ref[:, sl]).astype(BF16)
    n_chunks = wqvt_ref.shape[0] // TN_PROJ
    for c in range(n_chunks):
        sl = slice(c * TN_PROJ, (c + 1) * TN_PROJ)
        r = _dot_nt(wqvt_ref[sl, :], u)
        if c < n_chunks // 2:
            r = r * QK_SCALE
        qvt_ref[sl, :] = r.astype(BF16)
    ft_ref[...] = _dot_nt(wft_ref[...], u)


def _inproj(h, g, w_k, w_qvt, w_ft):
    t, d = h.shape
    nk, nqv, nf = w_k.shape[1], w_qvt.shape[0], w_ft.shape[0]
    assert TN_PROJ == BRANCH_W and nk == 3 * BRANCH_W and nqv == 6 * BRANCH_W
    return pl.pallas_call(
        _inproj_kernel,
        grid=(t // TM_DENSE,),
        in_specs=[pl.BlockSpec((TM_DENSE, d), lambda i: (i, 0)), _resident((1, d)),
                  _resident((d, nk)), _resident((nqv, d)), _resident((nf, d))],
        out_specs=[pl.BlockSpec((TM_DENSE, nk), lambda i: (i, 0)),
                   pl.BlockSpec((nqv, TM_DENSE), lambda i: (0, i)),
                   pl.BlockSpec((nf, TM_DENSE), lambda i: (0, i))],
        out_shape=[jax.ShapeDtypeStruct((t, nk), BF16), jax.ShapeDtypeStruct((nqv, t), BF16),
                   jax.ShapeDtypeStruct((nf, t), F32)],
        compiler_params=_params(1),
        name="inproj",
    )(h, g, w_k, w_qvt, w_ft)


def _decay_kernel(ft_ref, bias_ref, sel_ref, kc_ref):
    x = ft_ref[...] + bias_ref[...]
    x = jnp.minimum(x, 0.0) - jnp.log1p(jnp.exp(-jnp.abs(x)))
    nf, s = x.shape
    lane = lax.broadcasted_iota(jnp.int32, x.shape, 1)
    sh = 1
    while sh < s:
        x = x + jnp.where(lane >= sh, pltpu.roll(x, sh, axis=1), 0.0)
        sh *= 2
    parts = jnp.concatenate(_split3(x) + (jnp.zeros((LANES - N_SPLIT * nf, s), F32),), axis=0)
    kc_ref[0] = _dot(parts.T.astype(BF16), sel_ref[...]).astype(BF16)


def _decay(ft, bias, sel, b, s):
    nf = ft.shape[0]
    return pl.pallas_call(
        _decay_kernel,
        grid=(b,),
        in_specs=[pl.BlockSpec((nf, s), lambda i: (0, i)), _resident((nf, 1)),
                  _resident(sel.shape)],
        out_specs=pl.BlockSpec((1, s, BRANCH_W), lambda i: (i, 0, 0)),
        out_shape=jax.ShapeDtypeStruct((b, s, BRANCH_W), BF16),
        compiler_params=_params(1),
        name="decay",
    )(ft, bias, sel)


def _feat_iota(t):
    return lax.broadcasted_iota(jnp.int32, (LANES, t), 0)


def _group(n):
    return slice((n // 2) * LANES, (n // 2 + 1) * LANES)


def _chain_queries(qt_ref, n, feat=None):
    qt = qt_ref[_group(n), :]
    sub = _feat_iota(qt.shape[1])
    lo = (n % 2) * HEAD_DIM
    q = jnp.where((sub >= lo) & (sub < lo + HEAD_DIM), qt, jnp.zeros_like(qt))
    return q if feat is None else jnp.concatenate([q, feat], axis=0)


def _keys(off, width):
    return pl.ds(pl.multiple_of(off, width), width)


def _causal_blocks(i, block, reverse=False, alive=None):
    t = T_ATTN

    def pairs(p, _):
        off = (i // 2 - 1 - p if reverse else p) * 2 * t
        if alive is None:
            block(off, 2 * t, False)
        else:
            pl.when(alive())(lambda: block(off, 2 * t, False))
        return 0

    def single():
        odd = i % 2 == 1
        pl.when(odd if alive is None else odd & alive())(lambda: block((i - 1) * t, t, False))

    if reverse:
        block(i * t, t, True)
        single()
        lax.fori_loop(0, i // 2, pairs, 0)
    else:
        lax.fori_loop(0, i // 2, pairs, 0)
        single()
        block(i * t, t, True)


def _softmax_attend(i, key_operand, qs, vt_ref, m_scr, l_scr, acc_scr):
    t = T_ATTN
    m_scr[...] = jnp.full(m_scr.shape, NEG_BIG, F32)
    l_scr[...] = jnp.zeros(l_scr.shape, F32)
    acc_scr[...] = jnp.zeros(acc_scr.shape, F32)

    def block(off, width, diagonal):
        ks = _keys(off, width)
        ss = [_dot(key_operand(ks, n), qs[n]) for n in range(N_CHAINS)]
        if diagonal:
            key = lax.broadcasted_iota(jnp.int32, (width, t), 0)
            qry = lax.broadcasted_iota(jnp.int32, (width, t), 1)
        for n in range(N_CHAINS):
            s = jnp.where(key <= qry, ss[n], -jnp.inf) if diagonal else ss[n]
            m = m_scr[n]
            m_new = jnp.maximum(m, jnp.max(s, axis=0, keepdims=True))
            p = jnp.exp(s - m_new)
            alpha = jnp.exp(m - m_new)
            l_scr[n] = alpha * l_scr[n] + jnp.sum(p, axis=0, keepdims=True)
            acc_scr[n] = alpha * acc_scr[n] + _dot(vt_ref[_group(n), ks], p.astype(BF16))
            m_scr[n] = m_new

    _causal_blocks(i, block)
    return [acc_scr[n] * (1.0 / l_scr[n]) for n in range(N_CHAINS)]


def _softmax_scratch():
    row = pltpu.VMEM((N_CHAINS, 1, T_ATTN), F32)
    return [row, row, pltpu.VMEM((N_CHAINS, LANES, T_ATTN), F32)]


def _head_rows(a0, a1):
    sub = lax.broadcasted_iota(jnp.int32, a0.shape, 0)
    return jnp.where(sub < HEAD_DIM, a0, a1)


def _attn_specs(s, branch):
    nq = s // T_ATTN
    qt_spec = pl.BlockSpec((BRANCH_W, T_ATTN), lambda b, i: (branch, b * nq + i))
    k_spec = pl.BlockSpec((1, s, BRANCH_W), lambda b, i: (b, 0, branch))
    vt_spec = pl.BlockSpec((BRANCH_W, s), lambda b, i: (3 + branch, b))
    o_spec = pl.BlockSpec((1, T_ATTN, BRANCH_W), lambda b, i: (b, i, 0))
    return qt_spec, k_spec, vt_spec, o_spec


def _diff_kernel(slopes_ref, qt_ref, k_ref, pos_ref, vt_ref, lam_ref, subln_ref, o_ref,
                 m_scr, l_scr, acc_scr, *, lam_init):
    i = pl.program_id(1)
    t = T_ATTN
    sub = _feat_iota(t)
    qs = []
    for n in range(N_CHAINS):
        slope = slopes_ref[n // 2]
        feat = jnp.where(sub == 0, slope * POS_RADIX, jnp.where(sub == 1, slope, 0.0)).astype(BF16)
        qs.append(_chain_queries(qt_ref, n, feat))

    def key_operand(ks, n):
        return jnp.concatenate([k_ref[0, ks, _group(n)], pos_ref[0, ks, :]], axis=1)

    acc = _softmax_attend(i, key_operand, qs, vt_ref, m_scr, l_scr, acc_scr)

    lv = lam_ref[...]
    lam = (jnp.exp(jnp.sum(lv[0:1] * lv[1:2], axis=-1, keepdims=True))
           - jnp.exp(jnp.sum(lv[2:3] * lv[3:4], axis=-1, keepdims=True)) + lam_init)
    for g in range(N_GROUPS):
        o = (acc[2 * g] - lam * acc[2 * g + 1]).T
        o = _rms(o, subln_ref[...], SUBLN_EPS) * (1.0 - lam_init)
        o_ref[0, :, g * LANES:(g + 1) * LANES] = o.astype(o_ref.dtype)


def _diff_attn(k, qvt, pos, slopes, lam_rows, subln, lam_init):
    b, s, _ = k.shape
    qt_spec, k_spec, vt_spec, o_spec = _attn_specs(s, 0)
    return pl.pallas_call(
        functools.partial(_diff_kernel, lam_init=lam_init),
        grid=(b, s // T_ATTN),
        in_specs=[pl.BlockSpec(memory_space=pltpu.SMEM), qt_spec, k_spec, _resident(pos.shape),
                  vt_spec, _resident(lam_rows.shape), _resident(subln.shape)],
        out_specs=o_spec,
        out_shape=jax.ShapeDtypeStruct((b, s, BRANCH_W), BF16),
        scratch_shapes=_softmax_scratch(),
        compiler_params=_params(2),
        name="diff_attn",
    )(slopes, qvt, k, pos, qvt, lam_rows, subln)


def _fox_kernel(qt_ref, k_ref, kc_ref, vt_ref, o_ref, m_scr, l_scr, acc_scr):
    i = pl.program_id(1)
    t = T_ATTN
    sub = _feat_iota(t)
    qs = []
    for n in range(N_CHAINS):
        lo = N_SPLIT * (n % 2)
        feat = jnp.where((sub >= lo) & (sub < lo + N_SPLIT), -1.0, 0.0).astype(BF16)
        qs.append(_chain_queries(qt_ref, n, feat))

    def key_operand(ks, n):
        return jnp.concatenate([k_ref[0, ks, _group(n)], kc_ref[0, ks, _group(n)]], axis=1)

    acc = _softmax_attend(i, key_operand, qs, vt_ref, m_scr, l_scr, acc_scr)
    for g in range(N_GROUPS):
        o_ref[0, :, g * LANES:(g + 1) * LANES] = _head_rows(
            acc[2 * g], acc[2 * g + 1]).T.astype(o_ref.dtype)


def _fox_attn(k, qvt, kc):
    b, s, _ = k.shape
    qt_spec, k_spec, vt_spec, o_spec = _attn_specs(s, 1)
    kc_spec = pl.BlockSpec((1, s, BRANCH_W), lambda b_, i: (b_, 0, 0))
    return pl.pallas_call(
        _fox_kernel,
        grid=(b, s // T_ATTN),
        in_specs=[qt_spec, k_spec, kc_spec, vt_spec],
        out_specs=o_spec,
        out_shape=jax.ShapeDtypeStruct((b, s, BRANCH_W), BF16),
        scratch_shapes=_softmax_scratch(),
        compiler_params=_params(2),
        name="fox_attn",
    )(qvt, k, kc, qvt)


def _sb_kernel(qt_ref, k_ref, vt_ref, tri_ref, o_ref, run_scr, acc_scr, alive_ref):
    i = pl.program_id(1)
    t = T_ATTN
    qs = [_chain_queries(qt_ref, n) for n in range(N_CHAINS)]
    tri = tri_ref[...]
    run_scr[...] = jnp.zeros(run_scr.shape, F32)
    acc_scr[...] = jnp.zeros(acc_scr.shape, F32)

    def block(off, width, diagonal):
        zs = [_dot(k_ref[0, _keys(off, width), _group(n)], qs[n]) for n in range(N_CHAINS)]
        if diagonal:
            strict = (lax.broadcasted_iota(jnp.int32, (t, t), 0)
                      < lax.broadcasted_iota(jnp.int32, (t, t), 1))
        mid = []
        for n in range(N_CHAINS):
            z = zs[n]
            log_beta = jnp.minimum(z, 0.0) - jnp.log(1.0 + jnp.exp(jnp.minimum(z, -z)))
            log_keep = log_beta - z
            if diagonal:
                log_keep = jnp.where(strict, log_keep, 0.0)
            hi = log_keep.astype(BF16)
            lo = (log_keep - hi.astype(F32)).astype(BF16)
            tiles = []
            for u in range(width // t):
                rows = slice(u * t, (u + 1) * t)
                cs = _dot(tri, hi[rows]) + _dot(tri, lo[rows])
                tiles.append((log_beta[rows] + cs[:t], cs[t:t + 1]))
            mid.append(tiles)
        for n in range(N_CHAINS):
            run = run_scr[n]
            acc = acc_scr[n]
            for u in reversed(range(width // t)):
                w = jnp.exp(mid[n][u][0])
                if diagonal:
                    w = jnp.where(strict, w, 0.0)
                acc = acc + jnp.exp(run) * _dot(vt_ref[_group(n), _keys(off + u * t, t)],
                                                w.astype(BF16))
                run = run + mid[n][u][1]
            run_scr[n] = run
            acc_scr[n] = acc
            top = run if n == 0 else jnp.maximum(top, run)
        alive_ref[0] = (jnp.max(jnp.exp(top)) > 0.0).astype(jnp.int32)

    _causal_blocks(i, block, reverse=True, alive=lambda: alive_ref[0] > 0)
    for g in range(N_GROUPS):
        o_ref[0, :, g * LANES:(g + 1) * LANES] = _head_rows(
            acc_scr[2 * g], acc_scr[2 * g + 1]).T.astype(o_ref.dtype)


def _sb_attn(k, qvt, tri):
    b, s, _ = k.shape
    qt_spec, k_spec, vt_spec, o_spec = _attn_specs(s, 2)
    return pl.pallas_call(
        _sb_kernel,
        grid=(b, s // T_ATTN),
        in_specs=[qt_spec, k_spec, vt_spec, _resident(tri.shape)],
        out_specs=o_spec,
        out_shape=jax.ShapeDtypeStruct((b, s, BRANCH_W), BF16),
        scratch_shapes=_softmax_scratch()[1:] + [pltpu.SMEM((1,), jnp.int32)],
        compiler_params=_params(2),
        name="sb_attn",
    )(qvt, k, qvt, tri)


def _mixout_kernel(h_ref, g_ref, oa_ref, ob_ref, oc_ref, wg_ref, pa_ref, pb_ref, pc_ref, wo_ref,
                   o_ref):
    h = h_ref[...]
    d = h.shape[1]
    u = _rms(h, g_ref[...], EPS).astype(BF16)
    m = jnp.zeros(h.shape, F32)
    for n, (o_r, p_r) in enumerate(((oa_ref, pa_ref), (ob_ref, pb_ref), (oc_ref, pc_ref))):
        gate = jax.nn.sigmoid(_dot(u, wg_ref[:, n * d:(n + 1) * d]))
        m = m + gate * _dot(o_r[...], p_r[...])
    o_ref[...] = h + _dot(m.astype(BF16), wo_ref[...])


def _mixout(h, g, oa, ob, oc, wg, pa, pb, pc, wo):
    t, d = h.shape
    row = pl.BlockSpec((TM_DENSE, d), lambda i: (i, 0))
    orow = pl.BlockSpec((TM_DENSE, BRANCH_W), lambda i: (i, 0))
    return pl.pallas_call(
        _mixout_kernel,
        grid=(t // TM_DENSE,),
        in_specs=[row, _resident((1, d)), orow, orow, orow, _resident(wg.shape),
                  _resident(pa.shape), _resident(pb.shape), _resident(pc.shape),
                  _resident(wo.shape)],
        out_specs=row,
        out_shape=jax.ShapeDtypeStruct((t, d), F32),
        compiler_params=_params(1),
        name="mixout",
    )(h, g, oa, ob, oc, wg, pa, pb, pc, wo)


def _decay_selector():
    sel = [[0.0] * BRANCH_W for _ in range(LANES)]
    for head in range(N_FOX):
        for piece in range(N_SPLIT):
            sel[piece * BF16_ROWS + head][(head // 2) * LANES + N_SPLIT * (head % 2) + piece] = 1.0
    return jnp.asarray(sel, BF16)


def kernel(x, ffn1_norm, ffn1_w1, ffn1_w3, ffn1_w2, mix_norm, w_in, forget_bias, diff_lq1, diff_lk1,
           diff_lq2, diff_lk2, diff_subln, proj_a, proj_b, proj_c, w_out, ffn2_norm, ffn2_w1,
           ffn2_w3, ffn2_w2, final_norm):
    b, s, d = x.shape
    depth = w_in.shape[0]
    assert s % T_ATTN == 0 and (b * s) % TM_DENSE == 0 and s <= POS_RADIX * POS_RADIX
    bw = BRANCH_W
    f_lo = 6 * bw
    c_lo = f_lo + N_FOX
    g_lo = c_lo + 3 * bw

    slopes = jnp.asarray([2.0 ** (-8.0 * (hh + 1) / N_GROUPS) for hh in range(N_GROUPS)], F32)
    ridx = lax.broadcasted_iota(jnp.int32, (T_ATTN + BF16_ROWS, T_ATTN), 0)
    cidx = lax.broadcasted_iota(jnp.int32, (T_ATTN + BF16_ROWS, T_ATTN), 1)
    tri = ((cidx > ridx) | (ridx == T_ATTN)).astype(BF16)
    kpos = lax.broadcasted_iota(jnp.int32, (s, LANES), 0)
    plane = lax.broadcasted_iota(jnp.int32, (s, LANES), 1)
    pos = jnp.where(plane == 0, kpos // POS_RADIX,
                    jnp.where(plane == 1, kpos % POS_RADIX, 0)).astype(BF16)[None]
    sel = _decay_selector()
    row = lambda a: a.reshape(1, -1).astype(F32)

    h = x.reshape(b * s, d)
    for l in range(depth):
        bf = lambda a: a[l].astype(BF16)
        h = _ffn(h, row(ffn1_norm[l]), bf(ffn1_w1), bf(ffn1_w3), bf(ffn1_w2), row(final_norm),
                 final=False)

        w = w_in[l]
        col = lambda lo: w[:, lo:lo + bw]
        w_k = jnp.concatenate([col(bw), col(4 * bw), col(c_lo + bw)], axis=1).astype(BF16)
        w_qvt = jnp.concatenate([col(0), col(3 * bw), col(c_lo),
                                 col(2 * bw), col(5 * bw), col(c_lo + 2 * bw)], axis=1).T.astype(BF16)
        w_ft = jnp.zeros((BF16_ROWS, d), BF16).at[:N_FOX].set(w[:, f_lo:c_lo].T.astype(BF16))
        k, qvt, ft = _inproj(h, row(mix_norm[l]), w_k, w_qvt, w_ft)
        k = k.reshape(b, s, -1)

        bias = jnp.zeros((BF16_ROWS, 1), F32).at[:N_FOX, 0].set(forget_bias[l])
        kc = _decay(ft, bias, sel, b, s)

        lam_init = 0.8 - 0.6 * math.exp(-0.3 * l)
        lam_rows = jnp.zeros((8, LANES), F32).at[:4, :HEAD_DIM].set(
            jnp.stack([diff_lq1[l], diff_lk1[l], diff_lq2[l], diff_lk2[l]]))
        o_a = _diff_attn(k, qvt, pos, slopes, lam_rows, row(diff_subln[l]), lam_init)
        o_b = _fox_attn(k, qvt, kc)
        o_c = _sb_attn(k, qvt, tri)

        flat = lambda a: a.reshape(b * s, bw)
        h = _mixout(h, row(mix_norm[l]), flat(o_a), flat(o_b), flat(o_c), w[:, g_lo:].astype(BF16),
                    bf(proj_a), bf(proj_b), bf(proj_c), bf(w_out))

        h = _ffn(h, row(ffn2_norm[l]), bf(ffn2_w1), bf(ffn2_w3), bf(ffn2_w2), row(final_norm),
                 final=(l == depth - 1))
    return h.reshape(b, s, d)
```

```python
import functools
import math

import jax
import jax.numpy as jnp
from jax import lax
from jax.experimental import pallas as pl
from jax.experimental.pallas import tpu as pltpu

F32 = jnp.float32
BF16 = jnp.bfloat16

EPS = 1e-6
SUBLN_EPS = 1e-5
HEAD_DIM = 64
LANES = 128
BF16_ROWS = 16
N_GROUPS = 4
N_CHAINS = 2 * N_GROUPS
BRANCH_W = N_GROUPS * LANES
N_FOX = 8
N_SPLIT = 3
QK_SCALE = 1.0 / math.sqrt(HEAD_DIM)
POS_RADIX = 256
VMEM_LIMIT = 56 * 1024 * 1024

TM_DENSE = 512
TF_FFN = 256
TN_PROJ = 512
T_ATTN = 256
NEG_BIG = -1e30


def _rms(x, g, eps):
    return x * lax.rsqrt(jnp.mean(x * x, axis=-1, keepdims=True) + eps) * g


def _dot(a, b):
    return jnp.dot(a, b, preferred_element_type=F32)


def _dot_nt(a, b):
    return lax.dot_general(a, b, (((1,), (1,)), ((), ())), preferred_element_type=F32)


def _resident(shape):
    nd = len(shape)
    return pl.BlockSpec(shape, lambda *_: (0,) * nd, pipeline_mode=pl.Buffered(1))


def _params(n_grid):
    return pltpu.CompilerParams(dimension_semantics=("arbitrary",) * n_grid,
                                vmem_limit_bytes=VMEM_LIMIT)


def _split3(x):
    hi = x.astype(BF16).astype(F32)
    r = x - hi
    mid = r.astype(BF16).astype(F32)
    return hi, mid, r - mid


def _ffn_kernel(h_ref, g_ref, w1_ref, w3_ref, w2_ref, fg_ref, o_ref, *, final):
    h = h_ref[...]
    nb = _rms(h, g_ref[...], EPS).astype(BF16)
    acc = jnp.zeros(h.shape, F32)
    for c in range(w1_ref.shape[1] // TF_FFN):
        sl = slice(c * TF_FFN, (c + 1) * TF_FFN)
        a = _dot(nb, w1_ref[:, sl])
        b = _dot(nb, w3_ref[:, sl])
        t = (a * jax.nn.sigmoid(a) * b).astype(BF16)
        acc = acc + _dot(t, w2_ref[sl, :])
    out = h + 0.5 * acc
    if final:
        out = _rms(out, fg_ref[...], EPS)
    o_ref[...] = out


def _ffn(h, g, w1, w3, w2, fg, final):
    t, d = h.shape
    ff = w1.shape[1]
    assert t % TM_DENSE == 0 and ff % TF_FFN == 0
    row = pl.BlockSpec((TM_DENSE, d), lambda i: (i, 0))
    return pl.pallas_call(
        functools.partial(_ffn_kernel, final=final),
        grid=(t // TM_DENSE,),
        in_specs=[row, _resident((1, d)), _resident((d, ff)), _resident((d, ff)),
                  _resident((ff, d)), _resident((1, d))],
        out_specs=row,
        out_shape=jax.ShapeDtypeStruct((t, d), F32),
        compiler_params=_params(1),
        name="ffn",
    )(h, g, w1, w3, w2, fg)


def _inproj_kernel(h_ref, g_ref, wk_ref, wqvt_ref, wft_ref, k_ref, qvt_ref, ft_ref):
    u = _rms(h_ref[...], g_ref[...], EPS).astype(BF16)
    for c in range(wk_ref.shape[1] // TN_PROJ):
        sl = slice(c * TN_PROJ, (c + 1) * TN_PROJ)
        k_ref[:, sl] = _dot(u, wk_ref[:, sl]).astype(BF16)
    n_chunks = wqvt_ref.shape[0] // TN_PROJ
    for c in range(n_chunks):
        sl = slice(c * TN_PROJ, (c + 1) * TN_PROJ)
        r = _dot_nt(wqvt_ref[sl, :], u)
        if c < n_chunks // 2:
            r = r * QK_SCALE
        qvt_ref[sl, :] = r.astype(BF16)
    ft_ref[...] = _dot_nt(wft_ref[...], u)


def _inproj(h, g, w_k, w_qvt, w_ft):
    t, d = h.shape
    nk, nqv, nf = w_k.shape[1], w_qvt.shape[0], w_ft.shape[0]
    assert TN_PROJ == BRANCH_W and nk == 3 * BRANCH_W and nqv == 6 * BRANCH_W
    return pl.pallas_call(
        _inproj_kernel,
        grid=(t // TM_DENSE,),
        in_specs=[pl.BlockSpec((TM_DENSE, d), lambda i: (i, 0)), _resident((1, d)),
                  _resident((d, nk)), _resident((nqv, d)), _resident((nf, d))],
        out_specs=[pl.BlockSpec((TM_DENSE, nk), lambda i: (i, 0)),
                   pl.BlockSpec((nqv, TM_DENSE), lambda i: (0, i)),
                   pl.BlockSpec((nf, TM_DENSE), lambda i: (0, i))],
        out_shape=[jax.ShapeDtypeStruct((t, nk), BF16), jax.ShapeDtypeStruct((nqv, t), BF16),
                   jax.ShapeDtypeStruct((nf, t), F32)],
        compiler_params=_params(1),
        name="inproj",
    )(h, g, w_k, w_qvt, w_ft)


def _decay_kernel(ft_ref, bias_ref, sel_ref, kc_ref):
    x = ft_ref[...] + bias_ref[...]
    x = jnp.minimum(x, 0.0) - jnp.log1p(jnp.exp(-jnp.abs(x)))
    nf, s = x.shape
    lane = lax.broadcasted_iota(jnp.int32, x.shape, 1)
    sh = 1
    while sh < s:
        x = x + jnp.where(lane >= sh, pltpu.roll(x, sh, axis=1), 0.0)
        sh *= 2
    parts = jnp.concatenate(_split3(x) + (jnp.zeros((LANES - N_SPLIT * nf, s), F32),), axis=0)
    kc_ref[0] = _dot(parts.T.astype(BF16), sel_ref[...]).astype(BF16)


def _decay(ft, bias, sel, b, s):
    nf = ft.shape[0]
    return pl.pallas_call(
        _decay_kernel,
        grid=(b,),
        in_specs=[pl.BlockSpec((nf, s), lambda i: (0, i)), _resident((nf, 1)),
                  _resident(sel.shape)],
        out_specs=pl.BlockSpec((1, s, BRANCH_W), lambda i: (i, 0, 0)),
        out_shape=jax.ShapeDtypeStruct((b, s, BRANCH_W), BF16),
        compiler_params=_params(1),
        name="decay",
    )(ft, bias, sel)


def _feat_iota(t):
    return lax.broadcasted_iota(jnp.int32, (LANES, t), 0)


def _group(n):
    return slice((n // 2) * LANES, (n // 2 + 1) * LANES)


def _chain_queries(qt_ref, n, feat=None):
    qt = qt_ref[_group(n), :]
    sub = _feat_iota(qt.shape[1])
    lo = (n % 2) * HEAD_DIM
    q = jnp.where((sub >= lo) & (sub < lo + HEAD_DIM), qt, jnp.zeros_like(qt))
    return q if feat is None else jnp.concatenate([q, feat], axis=0)


def _keys(off, width):
    return pl.ds(pl.multiple_of(off, T_ATTN), width)


def _softmax_attend(i, key_operand, qs, vt_ref, s_scr, mtile_scr, mrun_scr, m_scr, l_scr, acc_scr):
    t = T_ATTN
    m_scr[...] = jnp.full(m_scr.shape, NEG_BIG, F32)
    mrun_scr[...] = jnp.full(mrun_scr.shape, NEG_BIG, F32)
    l_scr[...] = jnp.zeros(l_scr.shape, F32)
    acc_scr[...] = jnp.zeros(acc_scr.shape, F32)

    def score_stage(b, slot, diagonal):
        ks = _keys(b * t, t)
        if diagonal:
            causal = (lax.broadcasted_iota(jnp.int32, (t, t), 0)
                      <= lax.broadcasted_iota(jnp.int32, (t, t), 1))
        for n in range(N_CHAINS):
            s = _dot(key_operand(ks, n), qs[n])
            if diagonal:
                s = jnp.where(causal, s, -jnp.inf)
            s_scr[slot, n] = s
            m_new = jnp.maximum(mrun_scr[n], jnp.max(s, axis=0, keepdims=True))
            mrun_scr[n] = m_new
            mtile_scr[slot, n] = m_new

    def value_stage(b, slot):
        ks = _keys(b * t, t)
        for n in range(N_CHAINS):
            m_new = mtile_scr[slot, n]
            p = jnp.exp(s_scr[slot, n] - m_new)
            alpha = jnp.exp(m_scr[n] - m_new)
            l_scr[n] = alpha * l_scr[n] + jnp.sum(p, axis=0, keepdims=True)
            acc_scr[n] = alpha * acc_scr[n] + _dot(vt_ref[_group(n), ks], p.astype(BF16))
            m_scr[n] = m_new

    def two_tiles(p, _):
        score_stage(2 * p + 1, 1, False)
        value_stage(2 * p, 0)
        score_stage(2 * p + 2, 0, False)
        value_stage(2 * p + 1, 1)
        return 0

    @pl.when(i == 0)
    def _():
        score_stage(0, 0, True)
        value_stage(0, 0)

    @pl.when(i > 0)
    def _():
        score_stage(0, 0, False)
        lax.fori_loop(0, (i - 1) // 2, two_tiles, 0)

        @pl.when(i % 2 == 0)
        def _():
            score_stage(i - 1, 1, False)
            value_stage(i - 2, 0)
            score_stage(i, 0, True)
            value_stage(i - 1, 1)
            value_stage(i, 0)

        @pl.when(i % 2 == 1)
        def _():
            score_stage(i, 1, True)
            value_stage(i - 1, 0)
            value_stage(i, 1)

    return [acc_scr[n] * (1.0 / l_scr[n]) for n in range(N_CHAINS)]


def _chain_rows():
    return pltpu.VMEM((N_CHAINS, 1, T_ATTN), F32)


def _chain_accumulators():
    return pltpu.VMEM((N_CHAINS, LANES, T_ATTN), F32)


def _softmax_scratch():
    return [pltpu.VMEM((2, N_CHAINS, T_ATTN, T_ATTN), F32), pltpu.VMEM((2, N_CHAINS, 1, T_ATTN), F32),
            _chain_rows(), _chain_rows(), _chain_rows(), _chain_accumulators()]


def _head_rows(a0, a1):
    sub = lax.broadcasted_iota(jnp.int32, a0.shape, 0)
    return jnp.where(sub < HEAD_DIM, a0, a1)


def _attn_specs(s, branch):
    nq = s // T_ATTN
    qt_spec = pl.BlockSpec((BRANCH_W, T_ATTN), lambda b, i: (branch, b * nq + i))
    k_spec = pl.BlockSpec((1, s, BRANCH_W), lambda b, i: (b, 0, branch))
    vt_spec = pl.BlockSpec((BRANCH_W, s), lambda b, i: (3 + branch, b))
    o_spec = pl.BlockSpec((1, T_ATTN, BRANCH_W), lambda b, i: (b, i, 0))
    return qt_spec, k_spec, vt_spec, o_spec


def _diff_kernel(slopes_ref, qt_ref, k_ref, pos_ref, vt_ref, lam_ref, subln_ref, o_ref,
                 *softmax_scratch, lam_init):
    i = pl.program_id(1)
    t = T_ATTN
    sub = _feat_iota(t)
    qs = []
    for n in range(N_CHAINS):
        slope = slopes_ref[n // 2]
        feat = jnp.where(sub == 0, slope * POS_RADIX, jnp.where(sub == 1, slope, 0.0)).astype(BF16)
        qs.append(_chain_queries(qt_ref, n, feat))

    def key_operand(ks, n):
        return jnp.concatenate([k_ref[0, ks, _group(n)], pos_ref[0, ks, :]], axis=1)

    acc = _softmax_attend(i, key_operand, qs, vt_ref, *softmax_scratch)

    lv = lam_ref[...]
    lam = (jnp.exp(jnp.sum(lv[0:1] * lv[1:2], axis=-1, keepdims=True))
           - jnp.exp(jnp.sum(lv[2:3] * lv[3:4], axis=-1, keepdims=True)) + lam_init)
    for g in range(N_GROUPS):
        o = (acc[2 * g] - lam * acc[2 * g + 1]).T
        o = _rms(o, subln_ref[...], SUBLN_EPS) * (1.0 - lam_init)
        o_ref[0, :, g * LANES:(g + 1) * LANES] = o.astype(o_ref.dtype)


def _diff_attn(k, qvt, pos, slopes, lam_rows, subln, lam_init):
    b, s, _ = k.shape
    qt_spec, k_spec, vt_spec, o_spec = _attn_specs(s, 0)
    return pl.pallas_call(
        functools.partial(_diff_kernel, lam_init=lam_init),
        grid=(b, s // T_ATTN),
        in_specs=[pl.BlockSpec(memory_space=pltpu.SMEM), qt_spec, k_spec, _resident(pos.shape),
                  vt_spec, _resident(lam_rows.shape), _resident(subln.shape)],
        out_specs=o_spec,
        out_shape=jax.ShapeDtypeStruct((b, s, BRANCH_W), BF16),
        scratch_shapes=_softmax_scratch(),
        compiler_params=_params(2),
        name="diff_attn",
    )(slopes, qvt, k, pos, qvt, lam_rows, subln)


def _fox_kernel(qt_ref, k_ref, kc_ref, vt_ref, o_ref, *softmax_scratch):
    i = pl.program_id(1)
    t = T_ATTN
    sub = _feat_iota(t)
    qs = []
    for n in range(N_CHAINS):
        lo = N_SPLIT * (n % 2)
        feat = jnp.where((sub >= lo) & (sub < lo + N_SPLIT), -1.0, 0.0).astype(BF16)
        qs.append(_chain_queries(qt_ref, n, feat))

    def key_operand(ks, n):
        return jnp.concatenate([k_ref[0, ks, _group(n)], kc_ref[0, ks, _group(n)]], axis=1)

    acc = _softmax_attend(i, key_operand, qs, vt_ref, *softmax_scratch)
    for g in range(N_GROUPS):
        o_ref[0, :, g * LANES:(g + 1) * LANES] = _head_rows(
            acc[2 * g], acc[2 * g + 1]).T.astype(o_ref.dtype)


def _fox_attn(k, qvt, kc):
    b, s, _ = k.shape
    qt_spec, k_spec, vt_spec, o_spec = _attn_specs(s, 1)
    kc_spec = pl.BlockSpec((1, s, BRANCH_W), lambda b_, i: (b_, 0, 0))
    return pl.pallas_call(
        _fox_kernel,
        grid=(b, s // T_ATTN),
        in_specs=[qt_spec, k_spec, kc_spec, vt_spec],
        out_specs=o_spec,
        out_shape=jax.ShapeDtypeStruct((b, s, BRANCH_W), BF16),
        scratch_shapes=_softmax_scratch(),
        compiler_params=_params(2),
        name="fox_attn",
    )(qvt, k, kc, qvt)


def _sb_kernel(qt_ref, k_ref, vt_ref, tri_ref, o_ref, run_scr, acc_scr, alive_ref):
    i = pl.program_id(1)
    t = T_ATTN
    qs = [_chain_queries(qt_ref, n) for n in range(N_CHAINS)]
    tri = tri_ref[...]
    run_scr[...] = jnp.zeros(run_scr.shape, F32)
    acc_scr[...] = jnp.zeros(acc_scr.shape, F32)

    def block(off, width, diagonal):
        zs = [_dot(k_ref[0, _keys(off, width), _group(n)], qs[n]) for n in range(N_CHAINS)]
        if diagonal:
            strict = (lax.broadcasted_iota(jnp.int32, (t, t), 0)
                      < lax.broadcasted_iota(jnp.int32, (t, t), 1))
        mid = []
        for n in range(N_CHAINS):
            z = zs[n]
            log_beta = jnp.minimum(z, 0.0) - jnp.log(1.0 + jnp.exp(jnp.minimum(z, -z)))
            log_keep = log_beta - z
            if diagonal:
                log_keep = jnp.where(strict, log_keep, 0.0)
            hi = log_keep.astype(BF16)
            lo = (log_keep - hi.astype(F32)).astype(BF16)
            tiles = []
            for u in range(width // t):
                rows = slice(u * t, (u + 1) * t)
                cs = _dot(tri, hi[rows]) + _dot(tri, lo[rows])
                tiles.append((log_beta[rows] + cs[:t], cs[t:t + 1]))
            mid.append(tiles)
        for n in range(N_CHAINS):
            run = run_scr[n]
            acc = acc_scr[n]
            for u in reversed(range(width // t)):
                w = jnp.exp(mid[n][u][0])
                if diagonal:
                    w = jnp.where(strict, w, 0.0)
                acc = acc + jnp.exp(run) * _dot(vt_ref[_group(n), _keys(off + u * t, t)],
                                                w.astype(BF16))
                run = run + mid[n][u][1]
            run_scr[n] = run
            acc_scr[n] = acc
            top = run if n == 0 else jnp.maximum(top, run)
        alive_ref[0] = (jnp.max(jnp.exp(top)) > 0.0).astype(jnp.int32)

    alive = lambda: alive_ref[0] > 0
    block(i * t, t, True)
    pl.when((i >= 1) & alive())(lambda: block((i - 1) * t, t, False))

    def two_tiles(p, _):
        pl.when(alive())(lambda: block((i - 3 - 2 * p) * t, 2 * t, False))
        return 0

    lax.fori_loop(0, (i - 1) // 2, two_tiles, 0)
    pl.when((i >= 2) & (i % 2 == 0) & alive())(lambda: block(0, t, False))
    for g in range(N_GROUPS):
        o_ref[0, :, g * LANES:(g + 1) * LANES] = _head_rows(
            acc_scr[2 * g], acc_scr[2 * g + 1]).T.astype(o_ref.dtype)


def _sb_attn(k, qvt, tri):
    b, s, _ = k.shape
    qt_spec, k_spec, vt_spec, o_spec = _attn_specs(s, 2)
    return pl.pallas_call(
        _sb_kernel,
        grid=(b, s // T_ATTN),
        in_specs=[qt_spec, k_spec, vt_spec, _resident(tri.shape)],
        out_specs=o_spec,
        out_shape=jax.ShapeDtypeStruct((b, s, BRANCH_W), BF16),
        scratch_shapes=[_chain_rows(), _chain_accumulators(), pltpu.SMEM((1,), jnp.int32)],
        compiler_params=_params(2),
        name="sb_attn",
    )(qvt, k, qvt, tri)


def _mixout_kernel(h_ref, g_ref, oa_ref, ob_ref, oc_ref, wg_ref, pa_ref, pb_ref, pc_ref, wo_ref,
                   o_ref):
    h = h_ref[...]
    d = h.shape[1]
    u = _rms(h, g_ref[...], EPS).astype(BF16)
    m = jnp.zeros(h.shape, F32)
    for n, (o_r, p_r) in enumerate(((oa_ref, pa_ref), (ob_ref, pb_ref), (oc_ref, pc_ref))):
        gate = jax.nn.sigmoid(_dot(u, wg_ref[:, n * d:(n + 1) * d]))
        m = m + gate * _dot(o_r[...], p_r[...])
    o_ref[...] = h + _dot(m.astype(BF16), wo_ref[...])


def _mixout(h, g, oa, ob, oc, wg, pa, pb, pc, wo):
    t, d = h.shape
    row = pl.BlockSpec((TM_DENSE, d), lambda i: (i, 0))
    orow = pl.BlockSpec((TM_DENSE, BRANCH_W), lambda i: (i, 0))
    return pl.pallas_call(
        _mixout_kernel,
        grid=(t // TM_DENSE,),
        in_specs=[row, _resident((1, d)), orow, orow, orow, _resident(wg.shape),
                  _resident(pa.shape), _resident(pb.shape), _resident(pc.shape),
                  _resident(wo.shape)],
        out_specs=row,
        out_shape=jax.ShapeDtypeStruct((t, d), F32),
        compiler_params=_params(1),
        name="mixout",
    )(h, g, oa, ob, oc, wg, pa, pb, pc, wo)


def _decay_selector():
    sel = [[0.0] * BRANCH_W for _ in range(LANES)]
    for head in range(N_FOX):
        for piece in range(N_SPLIT):
            sel[piece * BF16_ROWS + head][(head // 2) * LANES + N_SPLIT * (head % 2) + piece] = 1.0
    return jnp.asarray(sel, BF16)


def kernel(x, ffn1_norm, ffn1_w1, ffn1_w3, ffn1_w2, mix_norm, w_in, forget_bias, diff_lq1, diff_lk1,
           diff_lq2, diff_lk2, diff_subln, proj_a, proj_b, proj_c, w_out, ffn2_norm, ffn2_w1,
           ffn2_w3, ffn2_w2, final_norm):
    b, s, d = x.shape
    depth = w_in.shape[0]
    assert s % T_ATTN == 0 and (b * s) % TM_DENSE == 0 and s <= POS_RADIX * POS_RADIX
    bw = BRANCH_W
    f_lo = 6 * bw
    c_lo = f_lo + N_FOX
    g_lo = c_lo + 3 * bw

    slopes = jnp.asarray([2.0 ** (-8.0 * (hh + 1) / N_GROUPS) for hh in range(N_GROUPS)], F32)
    ridx = lax.broadcasted_iota(jnp.int32, (T_ATTN + BF16_ROWS, T_ATTN), 0)
    cidx = lax.broadcasted_iota(jnp.int32, (T_ATTN + BF16_ROWS, T_ATTN), 1)
    tri = ((cidx > ridx) | (ridx == T_ATTN)).astype(BF16)
    kpos = lax.broadcasted_iota(jnp.int32, (s, LANES), 0)
    plane = lax.broadcasted_iota(jnp.int32, (s, LANES), 1)
    pos = jnp.where(plane == 0, kpos // POS_RADIX,
                    jnp.where(plane == 1, kpos % POS_RADIX, 0)).astype(BF16)[None]
    sel = _decay_selector()
    row = lambda a: a.reshape(1, -1).astype(F32)

    h = x.reshape(b * s, d)
    for l in range(depth):
        bf = lambda a: a[l].astype(BF16)
        h = _ffn(h, row(ffn1_norm[l]), bf(ffn1_w1), bf(ffn1_w3), bf(ffn1_w2), row(final_norm),
                 final=False)

        w = w_in[l]
        col = lambda lo: w[:, lo:lo + bw]
        w_k = jnp.concatenate([col(bw), col(4 * bw), col(c_lo + bw)], axis=1).astype(BF16)
        w_qvt = jnp.concatenate([col(0), col(3 * bw), col(c_lo),
                                 col(2 * bw), col(5 * bw), col(c_lo + 2 * bw)], axis=1).T.astype(BF16)
        w_ft = jnp.zeros((BF16_ROWS, d), BF16).at[:N_FOX].set(w[:, f_lo:c_lo].T.astype(BF16))
        k, qvt, ft = _inproj(h, row(mix_norm[l]), w_k, w_qvt, w_ft)
        k = k.reshape(b, s, -1)

        bias = jnp.zeros((BF16_ROWS, 1), F32).at[:N_FOX, 0].set(forget_bias[l])
        kc = _decay(ft, bias, sel, b, s)

        lam_init = 0.8 - 0.6 * math.exp(-0.3 * l)
        lam_rows = jnp.zeros((8, LANES), F32).at[:4, :HEAD_DIM].set(
            jnp.stack([diff_lq1[l], diff_lk1[l], diff_lq2[l], diff_lk2[l]]))
        o_a = _diff_attn(k, qvt, pos, slopes, lam_rows, row(diff_subln[l]), lam_init)
        o_b = _fox_attn(k, qvt, kc)
        o_c = _sb_attn(k, qvt, tri)

        flat = lambda a: a.reshape(b * s, bw)
        h = _mixout(h, row(mix_norm[l]), flat(o_a), flat(o_b), flat(o_c), w[:, g_lo:].astype(BF16),
                    bf(proj_a), bf(proj_b), bf(proj_c), bf(w_out))

        h = _ffn(h, row(ffn2_norm[l]), bf(ffn2_w1), bf(ffn2_w3), bf(ffn2_w2), row(final_norm),
                 final=(l == depth - 1))
    return h.reshape(b, s, d)
```

```python
import functools
import math

import jax
import jax.numpy as jnp
from jax import lax
from jax.experimental import pallas as pl
from jax.experimental.pallas import tpu as pltpu

F32 = jnp.float32
BF16 = jnp.bfloat16

EPS = 1e-6
SUBLN_EPS = 1e-5
HEAD_DIM = 64
LANES = 128
BF16_ROWS = 16
N_GROUPS = 4
N_CHAINS = 2 * N_GROUPS
BRANCH_W = N_GROUPS * LANES
N_FOX = 8
N_SPLIT = 3
QK_SCALE = 1.0 / math.sqrt(HEAD_DIM)
POS_RADIX = 256
VMEM_LIMIT = 56 * 1024 * 1024

TM_DENSE = 512
TF_FFN = 256
TN_PROJ = 512
T_ATTN = 256
Q_PER_STEP = 2
NEG_BIG = -1e30


def _rms(x, g, eps):
    return x * lax.rsqrt(jnp.mean(x * x, axis=-1, keepdims=True) + eps) * g


def _dot(a, b):
    return jnp.dot(a, b, preferred_element_type=F32)


def _dot_nt(a, b):
    return lax.dot_general(a, b, (((1,), (1,)), ((), ())), preferred_element_type=F32)


def _resident(shape):
    nd = len(shape)
    return pl.BlockSpec(shape, lambda *_: (0,) * nd, pipeline_mode=pl.Buffered(1))


def _layer_resident(shape, layer):
    nd = len(shape) - 1
    return pl.BlockSpec((1,) + tuple(shape[1:]), lambda *_: (layer,) + (0,) * nd,
                        pipeline_mode=pl.Buffered(1))


def _params(n_grid):
    return pltpu.CompilerParams(dimension_semantics=("arbitrary",) * n_grid,
                                vmem_limit_bytes=VMEM_LIMIT)


def _split3(x):
    hi = x.astype(BF16).astype(F32)
    r = x - hi
    mid = r.astype(BF16).astype(F32)
    return hi, mid, r - mid


def _ffn_kernel(h_ref, g_ref, w1_ref, w3_ref, w2_ref, fg_ref, o_ref, *, final):
    h = h_ref[...]
    nb = _rms(h, g_ref[...], EPS).astype(BF16)
    acc = jnp.zeros(h.shape, F32)
    for c in range(w1_ref.shape[2] // TF_FFN):
        sl = slice(c * TF_FFN, (c + 1) * TF_FFN)
        a = _dot(nb, w1_ref[0, :, sl].astype(BF16))
        b = _dot(nb, w3_ref[0, :, sl].astype(BF16))
        t = (a * jax.nn.sigmoid(a) * b).astype(BF16)
        acc = acc + _dot(t, w2_ref[0, sl, :].astype(BF16))
    out = h + 0.5 * acc
    if final:
        out = _rms(out, fg_ref[...], EPS)
    o_ref[...] = out


def _ffn(h, g, w1, w3, w2, fg, layer, final):
    t, d = h.shape
    ff = w1.shape[2]
    assert t % TM_DENSE == 0 and ff % TF_FFN == 0
    row = pl.BlockSpec((TM_DENSE, d), lambda i: (i, 0))
    return pl.pallas_call(
        functools.partial(_ffn_kernel, final=final),
        grid=(t // TM_DENSE,),
        in_specs=[row, _resident((1, d)), _layer_resident(w1.shape, layer),
                  _layer_resident(w3.shape, layer), _layer_resident(w2.shape, layer),
                  _resident((1, d))],
        out_specs=row,
        out_shape=jax.ShapeDtypeStruct((t, d), F32),
        compiler_params=_params(1),
        name="ffn",
    )(h, g, w1, w3, w2, fg)


def _inproj_kernel(h_ref, g_ref, wk_ref, wqvt_ref, wft_ref, k_ref, qvt_ref, ft_ref):
    u = _rms(h_ref[...], g_ref[...], EPS).astype(BF16)
    for c in range(wk_ref.shape[1] // TN_PROJ):
        sl = slice(c * TN_PROJ, (c + 1) * TN_PROJ)
        k_ref[c] = _dot(u, wk_ref[:, sl]).astype(BF16)
    n_chunks = wqvt_ref.shape[0] // TN_PROJ
    for c in range(n_chunks):
        sl = slice(c * TN_PROJ, (c + 1) * TN_PROJ)
        r = _dot_nt(wqvt_ref[sl, :], u)
        if c < n_chunks // 2:
            r = r * QK_SCALE
        qvt_ref[sl, :] = r.astype(BF16)
    ft_ref[...] = _dot_nt(wft_ref[...], u)


def _inproj(h, g, w_k, w_qvt, w_ft):
    t, d = h.shape
    nk, nqv, nf = w_k.shape[1], w_qvt.shape[0], w_ft.shape[0]
    assert TN_PROJ == BRANCH_W and nk == 3 * BRANCH_W and nqv == 6 * BRANCH_W
    return pl.pallas_call(
        _inproj_kernel,
        grid=(t // TM_DENSE,),
        in_specs=[pl.BlockSpec((TM_DENSE, d), lambda i: (i, 0)), _resident((1, d)),
                  _resident((d, nk)), _resident((nqv, d)), _resident((nf, d))],
        out_specs=[pl.BlockSpec((nk // TN_PROJ, TM_DENSE, TN_PROJ), lambda i: (0, i, 0)),
                   pl.BlockSpec((nqv, TM_DENSE), lambda i: (0, i)),
                   pl.BlockSpec((nf, TM_DENSE), lambda i: (0, i))],
        out_shape=[jax.ShapeDtypeStruct((nk // TN_PROJ, t, TN_PROJ), BF16),
                   jax.ShapeDtypeStruct((nqv, t), BF16),
                   jax.ShapeDtypeStruct((nf, t), F32)],
        compiler_params=_params(1),
        name="inproj",
    )(h, g, w_k, w_qvt, w_ft)


def _decay_kernel(ft_ref, bias_ref, sel_ref, kc_ref):
    x = ft_ref[...] + bias_ref[...]
    x = jnp.minimum(x, 0.0) - jnp.log1p(jnp.exp(-jnp.abs(x)))
    nf, s = x.shape
    lane = lax.broadcasted_iota(jnp.int32, x.shape, 1)
    sh = 1
    while sh < s:
        x = x + jnp.where(lane >= sh, pltpu.roll(x, sh, axis=1), 0.0)
        sh *= 2
    parts = jnp.concatenate(_split3(x) + (jnp.zeros((LANES - N_SPLIT * nf, s), F32),), axis=0)
    kc_ref[0] = _dot(parts.T.astype(BF16), sel_ref[...]).astype(BF16)


def _decay(ft, bias, sel, b, s):
    nf = ft.shape[0]
    return pl.pallas_call(
        _decay_kernel,
        grid=(b,),
        in_specs=[pl.BlockSpec((nf, s), lambda i: (0, i)), _resident((nf, 1)),
                  _resident(sel.shape)],
        out_specs=pl.BlockSpec((1, s, BRANCH_W), lambda i: (i, 0, 0)),
        out_shape=jax.ShapeDtypeStruct((b, s, BRANCH_W), BF16),
        compiler_params=_params(1),
        name="decay",
    )(ft, bias, sel)


def _feat_iota(t):
    return lax.broadcasted_iota(jnp.int32, (LANES, t), 0)


def _group(n):
    return slice((n // 2) * LANES, (n // 2 + 1) * LANES)


def _chain_queries(qt_ref, half, n, feat=None):
    qt = qt_ref[_group(n), half * T_ATTN:(half + 1) * T_ATTN]
    sub = _feat_iota(qt.shape[1])
    lo = (n % 2) * HEAD_DIM
    q = jnp.where((sub >= lo) & (sub < lo + HEAD_DIM), qt, jnp.zeros_like(qt))
    return q if feat is None else jnp.concatenate([q, feat], axis=0)


def _keys(off, width):
    return pl.ds(pl.multiple_of(off, T_ATTN), width)


def _softmax_attend(i, odd, key_operand, qs, vt_ref, s_scr, mtile_scr, mrun_scr, m_scr, l_scr,
                    acc_scr):
    t = T_ATTN
    m_scr[...] = jnp.full(m_scr.shape, NEG_BIG, F32)
    mrun_scr[...] = jnp.full(mrun_scr.shape, NEG_BIG, F32)
    l_scr[...] = jnp.zeros(l_scr.shape, F32)
    acc_scr[...] = jnp.zeros(acc_scr.shape, F32)

    def score_stage(b, slot, diagonal):
        ks = _keys(b * t, t)
        if diagonal:
            causal = (lax.broadcasted_iota(jnp.int32, (t, t), 0)
                      <= lax.broadcasted_iota(jnp.int32, (t, t), 1))
        for n in range(N_CHAINS):
            s = _dot(key_operand(ks, n), qs[n])
            if diagonal:
                s = jnp.where(causal, s, -jnp.inf)
            s_scr[slot, n] = s
            m_new = jnp.maximum(mrun_scr[n], jnp.max(s, axis=0, keepdims=True))
            mrun_scr[n] = m_new
            mtile_scr[slot, n] = m_new

    def value_stage(b, slot):
        ks = _keys(b * t, t)
        for n in range(N_CHAINS):
            m_new = mtile_scr[slot, n]
            p = jnp.exp(s_scr[slot, n] - m_new)
            alpha = jnp.exp(m_scr[n] - m_new)
            l_scr[n] = alpha * l_scr[n] + jnp.sum(p, axis=0, keepdims=True)
            acc_scr[n] = alpha * acc_scr[n] + _dot(vt_ref[_group(n), ks], p.astype(BF16))
            m_scr[n] = m_new

    def two_tiles(p, _):
        score_stage(2 * p + 1, 1, False)
        value_stage(2 * p, 0)
        score_stage(2 * p + 2, 0, False)
        value_stage(2 * p + 1, 1)
        return 0

    if odd:
        score_stage(0, 0, False)
        lax.fori_loop(0, (i - 1) // 2, two_tiles, 0)
        score_stage(i, 1, True)
        value_stage(i - 1, 0)
        value_stage(i, 1)
    else:
        @pl.when(i == 0)
        def _():
            score_stage(0, 0, True)
            value_stage(0, 0)

        @pl.when(i > 0)
        def _():
            score_stage(0, 0, False)
            lax.fori_loop(0, (i - 1) // 2, two_tiles, 0)
            score_stage(i - 1, 1, False)
            value_stage(i - 2, 0)
            score_stage(i, 0, True)
            value_stage(i - 1, 1)
            value_stage(i, 0)

    return [acc_scr[n] * (1.0 / l_scr[n]) for n in range(N_CHAINS)]


def _chain_rows():
    return pltpu.VMEM((N_CHAINS, 1, T_ATTN), F32)


def _chain_accumulators():
    return pltpu.VMEM((N_CHAINS, LANES, T_ATTN), F32)


def _softmax_scratch():
    return [pltpu.VMEM((2, N_CHAINS, T_ATTN, T_ATTN), F32), pltpu.VMEM((2, N_CHAINS, 1, T_ATTN), F32),
            _chain_rows(), _chain_rows(), _chain_rows(), _chain_accumulators()]


def _head_rows(a0, a1):
    sub = lax.broadcasted_iota(jnp.int32, a0.shape, 0)
    return jnp.where(sub < HEAD_DIM, a0, a1)


def _attn_specs(s, branch):
    tq = Q_PER_STEP * T_ATTN
    steps = s // tq
    qt_spec = pl.BlockSpec((BRANCH_W, tq), lambda b, r: (branch, b * steps + r))
    k_spec = pl.BlockSpec((1, 1, s, BRANCH_W), lambda b, r: (branch, b, 0, 0))
    vt_spec = pl.BlockSpec((BRANCH_W, s), lambda b, r: (3 + branch, b))
    o_spec = pl.BlockSpec((1, tq, BRANCH_W), lambda b, r: (b, r, 0))
    return qt_spec, k_spec, vt_spec, o_spec


def _query_tiles():
    r = pl.program_id(1)
    return [(half, Q_PER_STEP * r + half, half % 2 == 1) for half in range(Q_PER_STEP)]


def _out_tile(o_ref, half, g):
    return o_ref.at[0, half * T_ATTN:(half + 1) * T_ATTN, g * LANES:(g + 1) * LANES]


def _diff_kernel(slopes_ref, qt_ref, k_ref, pos_ref, vt_ref, lam_ref, subln_ref, o_ref,
                 *softmax_scratch, lam_init):
    sub = _feat_iota(T_ATTN)
    feats = []
    for g in range(N_GROUPS):
        slope = slopes_ref[g]
        feats.append(jnp.where(sub == 0, slope * POS_RADIX,
                               jnp.where(sub == 1, slope, 0.0)).astype(BF16))

    def key_operand(ks, n):
        return jnp.concatenate([k_ref[0, 0, ks, _group(n)], pos_ref[0, ks, :]], axis=1)

    lv = lam_ref[...]
    lam = (jnp.exp(jnp.sum(lv[0:1] * lv[1:2], axis=-1, keepdims=True))
           - jnp.exp(jnp.sum(lv[2:3] * lv[3:4], axis=-1, keepdims=True)) + lam_init)
    for half, i, odd in _query_tiles():
        qs = [_chain_queries(qt_ref, half, n, feats[n // 2]) for n in range(N_CHAINS)]
        acc = _softmax_attend(i, odd, key_operand, qs, vt_ref, *softmax_scratch)
        for g in range(N_GROUPS):
            o = (acc[2 * g] - lam * acc[2 * g + 1]).T
            o = _rms(o, subln_ref[...], SUBLN_EPS) * (1.0 - lam_init)
            _out_tile(o_ref, half, g)[...] = o.astype(o_ref.dtype)


def _diff_attn(k, qvt, pos, slopes, lam_rows, subln, lam_init):
    _, b, s, _ = k.shape
    qt_spec, k_spec, vt_spec, o_spec = _attn_specs(s, 0)
    return pl.pallas_call(
        functools.partial(_diff_kernel, lam_init=lam_init),
        grid=(b, s // (Q_PER_STEP * T_ATTN)),
        in_specs=[pl.BlockSpec(memory_space=pltpu.SMEM), qt_spec, k_spec, _resident(pos.shape),
                  vt_spec, _resident(lam_rows.shape), _resident(subln.shape)],
        out_specs=o_spec,
        out_shape=jax.ShapeDtypeStruct((b, s, BRANCH_W), BF16),
        scratch_shapes=_softmax_scratch(),
        compiler_params=_params(2),
        name="diff_attn",
    )(slopes, qvt, k, pos, qvt, lam_rows, subln)


def _fox_kernel(qt_ref, k_ref, kc_ref, vt_ref, o_ref, *softmax_scratch):
    sub = _feat_iota(T_ATTN)
    feats = [jnp.where((sub >= N_SPLIT * c) & (sub < N_SPLIT * (c + 1)), -1.0, 0.0).astype(BF16)
             for c in range(2)]

    def key_operand(ks, n):
        return jnp.concatenate([k_ref[0, 0, ks, _group(n)], kc_ref[0, ks, _group(n)]], axis=1)

    for half, i, odd in _query_tiles():
        qs = [_chain_queries(qt_ref, half, n, feats[n % 2]) for n in range(N_CHAINS)]
        acc = _softmax_attend(i, odd, key_operand, qs, vt_ref, *softmax_scratch)
        for g in range(N_GROUPS):
            _out_tile(o_ref, half, g)[...] = _head_rows(
                acc[2 * g], acc[2 * g + 1]).T.astype(o_ref.dtype)


def _fox_attn(k, qvt, kc):
    _, b, s, _ = k.shape
    qt_spec, k_spec, vt_spec, o_spec = _attn_specs(s, 1)
    kc_spec = pl.BlockSpec((1, s, BRANCH_W), lambda b_, r: (b_, 0, 0))
    return pl.pallas_call(
        _fox_kernel,
        grid=(b, s // (Q_PER_STEP * T_ATTN)),
        in_specs=[qt_spec, k_spec, kc_spec, vt_spec],
        out_specs=o_spec,
        out_shape=jax.ShapeDtypeStruct((b, s, BRANCH_W), BF16),
        scratch_shapes=_softmax_scratch(),
        compiler_params=_params(2),
        name="fox_attn",
    )(qvt, k, kc, qvt)


def _sb_kernel(qt_ref, k_ref, vt_ref, tri_ref, o_ref, run_scr, acc_scr, alive_ref):
    tri = tri_ref[...]
    for half, i, _ in _query_tiles():
        qs = [_chain_queries(qt_ref, half, n) for n in range(N_CHAINS)]
        _sb_attend(i, qs, tri, k_ref, vt_ref, run_scr, acc_scr, alive_ref)
        for g in range(N_GROUPS):
            _out_tile(o_ref, half, g)[...] = _head_rows(
                acc_scr[2 * g], acc_scr[2 * g + 1]).T.astype(o_ref.dtype)


def _sb_attend(i, qs, tri, k_ref, vt_ref, run_scr, acc_scr, alive_ref):
    t = T_ATTN
    run_scr[...] = jnp.zeros(run_scr.shape, F32)
    acc_scr[...] = jnp.zeros(acc_scr.shape, F32)

    def block(off, width, diagonal):
        zs = [_dot(k_ref[0, 0, _keys(off, width), _group(n)], qs[n]) for n in range(N_CHAINS)]
        if diagonal:
            strict = (lax.broadcasted_iota(jnp.int32, (t, t), 0)
                      < lax.broadcasted_iota(jnp.int32, (t, t), 1))
        mid = []
        for n in range(N_CHAINS):
            z = zs[n]
            log_beta = jnp.minimum(z, 0.0) - jnp.log(1.0 + jnp.exp(jnp.minimum(z, -z)))
            log_keep = log_beta - z
            if diagonal:
                log_keep = jnp.where(strict, log_keep, 0.0)
            hi = log_keep.astype(BF16)
            lo = (log_keep - hi.astype(F32)).astype(BF16)
            tiles = []
            for u in range(width // t):
                rows = slice(u * t, (u + 1) * t)
                cs = _dot(tri, hi[rows]) + _dot(tri, lo[rows])
                tiles.append((log_beta[rows] + cs[:t], cs[t:t + 1]))
            mid.append(tiles)
        for n in range(N_CHAINS):
            run = run_scr[n]
            acc = acc_scr[n]
            for u in reversed(range(width // t)):
                w = jnp.exp(mid[n][u][0])
                if diagonal:
                    w = jnp.where(strict, w, 0.0)
                acc = acc + jnp.exp(run) * _dot(vt_ref[_group(n), _keys(off + u * t, t)],
                                                w.astype(BF16))
                run = run + mid[n][u][1]
            run_scr[n] = run
            acc_scr[n] = acc
            top = run if n == 0 else jnp.maximum(top, run)
        alive_ref[0] = (jnp.max(jnp.exp(top)) > 0.0).astype(jnp.int32)

    alive = lambda: alive_ref[0] > 0
    block(i * t, t, True)
    pl.when((i >= 1) & alive())(lambda: block((i - 1) * t, t, False))

    def two_tiles(p, _):
        pl.when(alive())(lambda: block((i - 3 - 2 * p) * t, 2 * t, False))
        return 0

    lax.fori_loop(0, (i - 1) // 2, two_tiles, 0)
    pl.when((i >= 2) & (i % 2 == 0) & alive())(lambda: block(0, t, False))


def _sb_attn(k, qvt, tri):
    _, b, s, _ = k.shape
    qt_spec, k_spec, vt_spec, o_spec = _attn_specs(s, 2)
    return pl.pallas_call(
        _sb_kernel,
        grid=(b, s // (Q_PER_STEP * T_ATTN)),
        in_specs=[qt_spec, k_spec, vt_spec, _resident(tri.shape)],
        out_specs=o_spec,
        out_shape=jax.ShapeDtypeStruct((b, s, BRANCH_W), BF16),
        scratch_shapes=[_chain_rows(), _chain_accumulators(), pltpu.SMEM((1,), jnp.int32)],
        compiler_params=_params(2),
        name="sb_attn",
    )(qvt, k, qvt, tri)


def _mixout_kernel(h_ref, g_ref, oa_ref, ob_ref, oc_ref, wg_ref, pa_ref, pb_ref, pc_ref, wo_ref,
                   o_ref):
    h = h_ref[...]
    d = h.shape[1]
    u = _rms(h, g_ref[...], EPS).astype(BF16)
    m = jnp.zeros(h.shape, F32)
    for n, (o_r, p_r) in enumerate(((oa_ref, pa_ref), (ob_ref, pb_ref), (oc_ref, pc_ref))):
        gate = jax.nn.sigmoid(_dot(u, wg_ref[:, n * d:(n + 1) * d]))
        m = m + gate * _dot(o_r[...], p_r[0].astype(BF16))
    o_ref[...] = h + _dot(m.astype(BF16), wo_ref[0].astype(BF16))


def _mixout(h, g, oa, ob, oc, wg, pa, pb, pc, wo, layer):
    t, d = h.shape
    row = pl.BlockSpec((TM_DENSE, d), lambda i: (i, 0))
    orow = pl.BlockSpec((TM_DENSE, BRANCH_W), lambda i: (i, 0))
    return pl.pallas_call(
        _mixout_kernel,
        grid=(t // TM_DENSE,),
        in_specs=[row, _resident((1, d)), orow, orow, orow, _resident(wg.shape),
                  _layer_resident(pa.shape, layer), _layer_resident(pb.shape, layer),
                  _layer_resident(pc.shape, layer), _layer_resident(wo.shape, layer)],
        out_specs=row,
        out_shape=jax.ShapeDtypeStruct((t, d), F32),
        compiler_params=_params(1),
        name="mixout",
    )(h, g, oa, ob, oc, wg, pa, pb, pc, wo)


def _decay_selector():
    sel = [[0.0] * BRANCH_W for _ in range(LANES)]
    for head in range(N_FOX):
        for piece in range(N_SPLIT):
            sel[piece * BF16_ROWS + head][(head // 2) * LANES + N_SPLIT * (head % 2) + piece] = 1.0
    return jnp.asarray(sel, BF16)


def kernel(x, ffn1_norm, ffn1_w1, ffn1_w3, ffn1_w2, mix_norm, w_in, forget_bias, diff_lq1, diff_lk1,
           diff_lq2, diff_lk2, diff_subln, proj_a, proj_b, proj_c, w_out, ffn2_norm, ffn2_w1,
           ffn2_w3, ffn2_w2, final_norm):
    b, s, d = x.shape
    depth = w_in.shape[0]
    assert s % (Q_PER_STEP * T_ATTN) == 0 and (b * s) % TM_DENSE == 0 and s <= POS_RADIX * POS_RADIX
    bw = BRANCH_W
    f_lo = 6 * bw
    c_lo = f_lo + N_FOX
    g_lo = c_lo + 3 * bw

    slopes = jnp.asarray([2.0 ** (-8.0 * (hh + 1) / N_GROUPS) for hh in range(N_GROUPS)], F32)
    ridx = lax.broadcasted_iota(jnp.int32, (T_ATTN + BF16_ROWS, T_ATTN), 0)
    cidx = lax.broadcasted_iota(jnp.int32, (T_ATTN + BF16_ROWS, T_ATTN), 1)
    tri = ((cidx > ridx) | (ridx == T_ATTN)).astype(BF16)
    kpos = lax.broadcasted_iota(jnp.int32, (s, LANES), 0)
    plane = lax.broadcasted_iota(jnp.int32, (s, LANES), 1)
    pos = jnp.where(plane == 0, kpos // POS_RADIX,
                    jnp.where(plane == 1, kpos % POS_RADIX, 0)).astype(BF16)[None]
    sel = _decay_selector()
    row = lambda a: a.reshape(1, -1).astype(F32)

    h = x.reshape(b * s, d)
    for l in range(depth):
        h = _ffn(h, row(ffn1_norm[l]), ffn1_w1, ffn1_w3, ffn1_w2, row(final_norm), l, final=False)

        w = w_in[l]
        col = lambda lo: w[:, lo:lo + bw]
        w_k = jnp.concatenate([col(bw), col(4 * bw), col(c_lo + bw)], axis=1).astype(BF16)
        w_qvt = jnp.concatenate([col(0), col(3 * bw), col(c_lo),
                                 col(2 * bw), col(5 * bw), col(c_lo + 2 * bw)], axis=1).T.astype(BF16)
        w_ft = jnp.zeros((BF16_ROWS, d), BF16).at[:N_FOX].set(w[:, f_lo:c_lo].T.astype(BF16))
        k, qvt, ft = _inproj(h, row(mix_norm[l]), w_k, w_qvt, w_ft)
        k = k.reshape(-1, b, s, bw)

        bias = jnp.zeros((BF16_ROWS, 1), F32).at[:N_FOX, 0].set(forget_bias[l])
        kc = _decay(ft, bias, sel, b, s)

        lam_init = 0.8 - 0.6 * math.exp(-0.3 * l)
        lam_rows = jnp.zeros((8, LANES), F32).at[:4, :HEAD_DIM].set(
            jnp.stack([diff_lq1[l], diff_lk1[l], diff_lq2[l], diff_lk2[l]]))
        o_a = _diff_attn(k, qvt, pos, slopes, lam_rows, row(diff_subln[l]), lam_init)
        o_b = _fox_attn(k, qvt, kc)
        o_c = _sb_attn(k, qvt, tri)

        flat = lambda a: a.reshape(b * s, bw)
        h = _mixout(h, row(mix_norm[l]), flat(o_a), flat(o_b), flat(o_c), w[:, g_lo:].astype(BF16),
                    proj_a, proj_b, proj_c, w_out, l)

        h = _ffn(h, row(ffn2_norm[l]), ffn2_w1, ffn2_w3, ffn2_w2, row(final_norm), l,
                 final=(l == depth - 1))
    return h.reshape(b, s, d)
```

```python
import functools
import math

import jax
import jax.numpy as jnp
from jax import lax
from jax.experimental import pallas as pl
from jax.experimental.pallas import tpu as pltpu

F32 = jnp.float32
BF16 = jnp.bfloat16

EPS = 1e-6
SUBLN_EPS = 1e-5
HEAD_DIM = 64
LANES = 128
BF16_ROWS = 16
N_GROUPS = 4
N_CHAINS = 2 * N_GROUPS
BRANCH_W = N_GROUPS * LANES
N_FOX = 8
N_SPLIT = 3
QK_SCALE = 1.0 / math.sqrt(HEAD_DIM)
LOG2_E = math.log2(math.e)
Q_SCALES = (QK_SCALE * LOG2_E, QK_SCALE * LOG2_E, QK_SCALE)
POS_RADIX = 256
VMEM_LIMIT = 56 * 1024 * 1024

TM_DENSE = 512
TF_FFN = 256
TN_PROJ = 512
T_ATTN = 256
Q_PER_STEP = 2
NEG_BIG = -1e30


def _rms(x, g, eps):
    return x * lax.rsqrt(jnp.mean(x * x, axis=-1, keepdims=True) + eps) * g


def _dot(a, b):
    return jnp.dot(a, b, preferred_element_type=F32)


def _dot_nt(a, b):
    return lax.dot_general(a, b, (((1,), (1,)), ((), ())), preferred_element_type=F32)


def _resident(shape):
    nd = len(shape)
    return pl.BlockSpec(shape, lambda *_: (0,) * nd, pipeline_mode=pl.Buffered(1))


def _layer_resident(shape, layer):
    nd = len(shape) - 1
    return pl.BlockSpec((1,) + tuple(shape[1:]), lambda *_: (layer,) + (0,) * nd,
                        pipeline_mode=pl.Buffered(1))


def _params(n_grid):
    return pltpu.CompilerParams(dimension_semantics=("arbitrary",) * n_grid,
                                vmem_limit_bytes=VMEM_LIMIT)


def _split3(x):
    hi = x.astype(BF16).astype(F32)
    r = x - hi
    mid = r.astype(BF16).astype(F32)
    return hi, mid, r - mid


def _ffn_kernel(h_ref, g_ref, w1_ref, w3_ref, w2_ref, fg_ref, o_ref, *, final):
    h = h_ref[...]
    nb = _rms(h, g_ref[...], EPS).astype(BF16)
    acc = jnp.zeros(h.shape, F32)
    for c in range(w1_ref.shape[2] // TF_FFN):
        sl = slice(c * TF_FFN, (c + 1) * TF_FFN)
        a = _dot(nb, w1_ref[0, :, sl].astype(BF16))
        b = _dot(nb, w3_ref[0, :, sl].astype(BF16))
        t = (a * jax.nn.sigmoid(a) * b).astype(BF16)
        acc = acc + _dot(t, w2_ref[0, sl, :].astype(BF16))
    out = h + 0.5 * acc
    if final:
        out = _rms(out, fg_ref[...], EPS)
    o_ref[...] = out


def _ffn(h, g, w1, w3, w2, fg, layer, final):
    t, d = h.shape
    ff = w1.shape[2]
    assert t % TM_DENSE == 0 and ff % TF_FFN == 0
    row = pl.BlockSpec((TM_DENSE, d), lambda i: (i, 0))
    return pl.pallas_call(
        functools.partial(_ffn_kernel, final=final),
        grid=(t // TM_DENSE,),
        in_specs=[row, _resident((1, d)), _layer_resident(w1.shape, layer),
                  _layer_resident(w3.shape, layer), _layer_resident(w2.shape, layer),
                  _resident((1, d))],
        out_specs=row,
        out_shape=jax.ShapeDtypeStruct((t, d), F32),
        compiler_params=_params(1),
        name="ffn",
    )(h, g, w1, w3, w2, fg)


def _inproj_kernel(h_ref, g_ref, wk_ref, wqvt_ref, wft_ref, k_ref, qvt_ref, ft_ref):
    u = _rms(h_ref[...], g_ref[...], EPS).astype(BF16)
    for c in range(wk_ref.shape[1] // TN_PROJ):
        sl = slice(c * TN_PROJ, (c + 1) * TN_PROJ)
        k_ref[c] = _dot(u, wk_ref[:, sl]).astype(BF16)
    n_chunks = wqvt_ref.shape[0] // TN_PROJ
    for c in range(n_chunks):
        sl = slice(c * TN_PROJ, (c + 1) * TN_PROJ)
        r = _dot_nt(wqvt_ref[sl, :], u)
        if c < n_chunks // 2:
            r = r * Q_SCALES[c]
        qvt_ref[sl, :] = r.astype(BF16)
    ft_ref[...] = _dot_nt(wft_ref[...], u)


def _inproj(h, g, w_k, w_qvt, w_ft):
    t, d = h.shape
    nk, nqv, nf = w_k.shape[1], w_qvt.shape[0], w_ft.shape[0]
    assert TN_PROJ == BRANCH_W and nk == 3 * BRANCH_W and nqv == 6 * BRANCH_W
    return pl.pallas_call(
        _inproj_kernel,
        grid=(t // TM_DENSE,),
        in_specs=[pl.BlockSpec((TM_DENSE, d), lambda i: (i, 0)), _resident((1, d)),
                  _resident((d, nk)), _resident((nqv, d)), _resident((nf, d))],
        out_specs=[pl.BlockSpec((nk // TN_PROJ, TM_DENSE, TN_PROJ), lambda i: (0, i, 0)),
                   pl.BlockSpec((nqv, TM_DENSE), lambda i: (0, i)),
                   pl.BlockSpec((nf, TM_DENSE), lambda i: (0, i))],
        out_shape=[jax.ShapeDtypeStruct((nk // TN_PROJ, t, TN_PROJ), BF16),
                   jax.ShapeDtypeStruct((nqv, t), BF16),
                   jax.ShapeDtypeStruct((nf, t), F32)],
        compiler_params=_params(1),
        name="inproj",
    )(h, g, w_k, w_qvt, w_ft)


def _decay_kernel(ft_ref, bias_ref, sel_ref, kc_ref):
    x = ft_ref[...] + bias_ref[...]
    x = jnp.minimum(x, 0.0) - jnp.log1p(jnp.exp(-jnp.abs(x)))
    nf, s = x.shape
    lane = lax.broadcasted_iota(jnp.int32, x.shape, 1)
    sh = 1
    while sh < s:
        x = x + jnp.where(lane >= sh, pltpu.roll(x, sh, axis=1), 0.0)
        sh *= 2
    parts = jnp.concatenate(_split3(x * LOG2_E) + (jnp.zeros((LANES - N_SPLIT * nf, s), F32),),
                            axis=0)
    kc_ref[0] = _dot(parts.T.astype(BF16), sel_ref[...]).astype(BF16)


def _decay(ft, bias, sel, b, s):
    nf = ft.shape[0]
    return pl.pallas_call(
        _decay_kernel,
        grid=(b,),
        in_specs=[pl.BlockSpec((nf, s), lambda i: (0, i)), _resident((nf, 1)),
                  _resident(sel.shape)],
        out_specs=pl.BlockSpec((1, s, BRANCH_W), lambda i: (i, 0, 0)),
        out_shape=jax.ShapeDtypeStruct((b, s, BRANCH_W), BF16),
        compiler_params=_params(1),
        name="decay",
    )(ft, bias, sel)


def _feat_iota(t):
    return lax.broadcasted_iota(jnp.int32, (LANES, t), 0)


def _group(n):
    return slice((n // 2) * LANES, (n // 2 + 1) * LANES)


def _chain_queries(qt_ref, half, n, feat=None):
    qt = qt_ref[_group(n), half * T_ATTN:(half + 1) * T_ATTN]
    sub = _feat_iota(qt.shape[1])
    lo = (n % 2) * HEAD_DIM
    q = jnp.where((sub >= lo) & (sub < lo + HEAD_DIM), qt, jnp.zeros_like(qt))
    return q if feat is None else jnp.concatenate([q, feat], axis=0)


def _keys(off, width):
    return pl.ds(pl.multiple_of(off, T_ATTN), width)


def _softmax_attend(i, odd, key_operand, qs, vt_ref, s_scr, mtile_scr, mrun_scr, m_scr, l_scr,
                    acc_scr):
    t = T_ATTN
    m_scr[...] = jnp.full(m_scr.shape, NEG_BIG, F32)
    mrun_scr[...] = jnp.full(mrun_scr.shape, NEG_BIG, F32)
    l_scr[...] = jnp.zeros(l_scr.shape, F32)
    acc_scr[...] = jnp.zeros(acc_scr.shape, F32)

    def score_stage(b, slot, diagonal):
        ks = _keys(b * t, t)
        if diagonal:
            causal = (lax.broadcasted_iota(jnp.int32, (t, t), 0)
                      <= lax.broadcasted_iota(jnp.int32, (t, t), 1))
        for n in range(N_CHAINS):
            s = _dot(key_operand(ks, n), qs[n])
            if diagonal:
                s = jnp.where(causal, s, -jnp.inf)
            s_scr[slot, n] = s
            m_new = jnp.maximum(mrun_scr[n], jnp.max(s, axis=0, keepdims=True))
            mrun_scr[n] = m_new
            mtile_scr[slot, n] = m_new

    def value_stage(b, slot):
        ks = _keys(b * t, t)
        for n in range(N_CHAINS):
            m_new = mtile_scr[slot, n]
            p = jnp.exp2(s_scr[slot, n] - m_new)
            alpha = jnp.exp2(m_scr[n] - m_new)
            l_scr[n] = alpha * l_scr[n] + jnp.sum(p, axis=0, keepdims=True)
            acc_scr[n] = alpha * acc_scr[n] + _dot(vt_ref[_group(n), ks], p.astype(BF16))
            m_scr[n] = m_new

    def two_tiles(p, _):
        score_stage(2 * p + 1, 1, False)
        value_stage(2 * p, 0)
        score_stage(2 * p + 2, 0, False)
        value_stage(2 * p + 1, 1)
        return 0

    if odd:
        score_stage(0, 0, False)
        lax.fori_loop(0, (i - 1) // 2, two_tiles, 0)
        score_stage(i, 1, True)
        value_stage(i - 1, 0)
        value_stage(i, 1)
    else:
        @pl.when(i == 0)
        def _():
            score_stage(0, 0, True)
            value_stage(0, 0)

        @pl.when(i > 0)
        def _():
            score_stage(0, 0, False)
            lax.fori_loop(0, (i - 1) // 2, two_tiles, 0)
            score_stage(i - 1, 1, False)
            value_stage(i - 2, 0)
            score_stage(i, 0, True)
            value_stage(i - 1, 1)
            value_stage(i, 0)

    return [acc_scr[n] * (1.0 / l_scr[n]) for n in range(N_CHAINS)]


def _chain_rows():
    return pltpu.VMEM((N_CHAINS, 1, T_ATTN), F32)


def _chain_accumulators():
    return pltpu.VMEM((N_CHAINS, LANES, T_ATTN), F32)


def _softmax_scratch():
    return [pltpu.VMEM((2, N_CHAINS, T_ATTN, T_ATTN), F32), pltpu.VMEM((2, N_CHAINS, 1, T_ATTN), F32),
            _chain_rows(), _chain_rows(), _chain_rows(), _chain_accumulators()]


def _head_rows(a0, a1):
    sub = lax.broadcasted_iota(jnp.int32, a0.shape, 0)
    return jnp.where(sub < HEAD_DIM, a0, a1)


def _attn_specs(s, branch):
    tq = Q_PER_STEP * T_ATTN
    steps = s // tq
    qt_spec = pl.BlockSpec((BRANCH_W, tq), lambda b, r: (branch, b * steps + r))
    k_spec = pl.BlockSpec((1, 1, s, BRANCH_W), lambda b, r: (branch, b, 0, 0))
    vt_spec = pl.BlockSpec((BRANCH_W, s), lambda b, r: (3 + branch, b))
    o_spec = pl.BlockSpec((1, tq, BRANCH_W), lambda b, r: (b, r, 0))
    return qt_spec, k_spec, vt_spec, o_spec


def _query_tiles():
    r = pl.program_id(1)
    return [(half, Q_PER_STEP * r + half, half % 2 == 1) for half in range(Q_PER_STEP)]


def _out_tile(o_ref, half, g):
    return o_ref.at[0, half * T_ATTN:(half + 1) * T_ATTN, g * LANES:(g + 1) * LANES]


def _diff_kernel(slopes_ref, qt_ref, k_ref, pos_ref, vt_ref, lam_ref, subln_ref, o_ref,
                 *softmax_scratch, lam_init):
    sub = _feat_iota(T_ATTN)
    feats = []
    for g in range(N_GROUPS):
        pieces = _split3(jnp.full(sub.shape, slopes_ref[g], F32) * LOG2_E)
        feat = jnp.zeros(sub.shape, F32)
        for j, piece in enumerate(pieces):
            feat = jnp.where(sub == 2 * j, piece * POS_RADIX, jnp.where(sub == 2 * j + 1, piece, feat))
        feats.append(feat.astype(BF16))

    def key_operand(ks, n):
        return jnp.concatenate([k_ref[0, 0, ks, _group(n)], pos_ref[0, ks, :]], axis=1)

    lv = lam_ref[...]
    lam = (jnp.exp(jnp.sum(lv[0:1] * lv[1:2], axis=-1, keepdims=True))
           - jnp.exp(jnp.sum(lv[2:3] * lv[3:4], axis=-1, keepdims=True)) + lam_init)
    for half, i, odd in _query_tiles():
        qs = [_chain_queries(qt_ref, half, n, feats[n // 2]) for n in range(N_CHAINS)]
        acc = _softmax_attend(i, odd, key_operand, qs, vt_ref, *softmax_scratch)
        for g in range(N_GROUPS):
            o = (acc[2 * g] - lam * acc[2 * g + 1]).T
            o = _rms(o, subln_ref[...], SUBLN_EPS) * (1.0 - lam_init)
            _out_tile(o_ref, half, g)[...] = o.astype(o_ref.dtype)


def _diff_attn(k, qvt, pos, slopes, lam_rows, subln, lam_init):
    _, b, s, _ = k.shape
    qt_spec, k_spec, vt_spec, o_spec = _attn_specs(s, 0)
    return pl.pallas_call(
        functools.partial(_diff_kernel, lam_init=lam_init),
        grid=(b, s // (Q_PER_STEP * T_ATTN)),
        in_specs=[pl.BlockSpec(memory_space=pltpu.SMEM), qt_spec, k_spec, _resident(pos.shape),
                  vt_spec, _resident(lam_rows.shape), _resident(subln.shape)],
        out_specs=o_spec,
        out_shape=jax.ShapeDtypeStruct((b, s, BRANCH_W), BF16),
        scratch_shapes=_softmax_scratch(),
        compiler_params=_params(2),
        name="diff_attn",
    )(slopes, qvt, k, pos, qvt, lam_rows, subln)


def _fox_kernel(qt_ref, k_ref, kc_ref, vt_ref, o_ref, *softmax_scratch):
    sub = _feat_iota(T_ATTN)
    feats = [jnp.where((sub >= N_SPLIT * c) & (sub < N_SPLIT * (c + 1)), -1.0, 0.0).astype(BF16)
             for c in range(2)]

    def key_operand(ks, n):
        return jnp.concatenate([k_ref[0, 0, ks, _group(n)], kc_ref[0, ks, _group(n)]], axis=1)

    for half, i, odd in _query_tiles():
        qs = [_chain_queries(qt_ref, half, n, feats[n % 2]) for n in range(N_CHAINS)]
        acc = _softmax_attend(i, odd, key_operand, qs, vt_ref, *softmax_scratch)
        for g in range(N_GROUPS):
            _out_tile(o_ref, half, g)[...] = _head_rows(
                acc[2 * g], acc[2 * g + 1]).T.astype(o_ref.dtype)


def _fox_attn(k, qvt, kc):
    _, b, s, _ = k.shape
    qt_spec, k_spec, vt_spec, o_spec = _attn_specs(s, 1)
    kc_spec = pl.BlockSpec((1, s, BRANCH_W), lambda b_, r: (b_, 0, 0))
    return pl.pallas_call(
        _fox_kernel,
        grid=(b, s // (Q_PER_STEP * T_ATTN)),
        in_specs=[qt_spec, k_spec, kc_spec, vt_spec],
        out_specs=o_spec,
        out_shape=jax.ShapeDtypeStruct((b, s, BRANCH_W), BF16),
        scratch_shapes=_softmax_scratch(),
        compiler_params=_params(2),
        name="fox_attn",
    )(qvt, k, kc, qvt)


def _sb_kernel(qt_ref, k_ref, vt_ref, tri_ref, o_ref, run_scr, acc_scr, alive_ref):
    tri = tri_ref[...]
    for half, i, odd in _query_tiles():
        qs = [_chain_queries(qt_ref, half, n) for n in range(N_CHAINS)]
        _sb_attend(i, odd, qs, tri, k_ref, vt_ref, run_scr, acc_scr, alive_ref)
        for g in range(N_GROUPS):
            _out_tile(o_ref, half, g)[...] = _head_rows(
                acc_scr[2 * g], acc_scr[2 * g + 1]).T.astype(o_ref.dtype)


def _sb_attend(i, odd, qs, tri, k_ref, vt_ref, run_scr, acc_scr, alive_ref):
    t = T_ATTN
    run_scr[...] = jnp.zeros(run_scr.shape, F32)
    acc_scr[...] = jnp.zeros(acc_scr.shape, F32)

    def block(off, width, diagonal):
        zs = [_dot(k_ref[0, 0, _keys(off, width), _group(n)], qs[n]) for n in range(N_CHAINS)]
        if diagonal:
            strict = (lax.broadcasted_iota(jnp.int32, (t, t), 0)
                      < lax.broadcasted_iota(jnp.int32, (t, t), 1))
        n_tiles = width // t
        mid = []
        for n in range(N_CHAINS):
            tiles = []
            for u in range(n_tiles):
                z = zs[n][u * t:(u + 1) * t]
                log_beta = jnp.minimum(z, 0.0) - jnp.log(1.0 + jnp.exp(jnp.minimum(z, -z)))
                log_keep = log_beta - z
                if diagonal and u == n_tiles - 1:
                    log_keep = jnp.where(strict, log_keep, 0.0)
                hi = log_keep.astype(BF16)
                lo = (log_keep - hi.astype(F32)).astype(BF16)
                cs = _dot(tri, hi) + _dot(tri, lo)
                tiles.append((log_beta + cs[:t], cs[t:t + 1]))
            mid.append(tiles)
        for n in range(N_CHAINS):
            run = run_scr[n]
            acc = acc_scr[n]
            for u in reversed(range(n_tiles)):
                w = jnp.exp(mid[n][u][0])
                if diagonal and u == n_tiles - 1:
                    w = jnp.where(strict, w, 0.0)
                acc = acc + jnp.exp(run) * _dot(vt_ref[_group(n), _keys(off + u * t, t)],
                                                w.astype(BF16))
                run = run + mid[n][u][1]
            run_scr[n] = run
            acc_scr[n] = acc
            top = run if n == 0 else jnp.maximum(top, run)
        alive_ref[0] = (jnp.max(jnp.exp(top)) > 0.0).astype(jnp.int32)

    alive = lambda: alive_ref[0] > 0
    if odd:
        block((i - 1) * t, 2 * t, True)
    else:
        pl.when(i == 0)(lambda: block(0, t, True))
        pl.when(i > 0)(lambda: block((i - 1) * t, 2 * t, True))

    def two_tiles(p, _):
        pl.when(alive())(lambda: block((i - 3 - 2 * p) * t, 2 * t, False))
        return 0

    lax.fori_loop(0, (i - 1) // 2, two_tiles, 0)
    pl.when((i >= 2) & (i % 2 == 0) & alive())(lambda: block(0, t, False))


def _sb_attn(k, qvt, tri):
    _, b, s, _ = k.shape
    qt_spec, k_spec, vt_spec, o_spec = _attn_specs(s, 2)
    return pl.pallas_call(
        _sb_kernel,
        grid=(b, s // (Q_PER_STEP * T_ATTN)),
        in_specs=[qt_spec, k_spec, vt_spec, _resident(tri.shape)],
        out_specs=o_spec,
        out_shape=jax.ShapeDtypeStruct((b, s, BRANCH_W), BF16),
        scratch_shapes=[_chain_rows(), _chain_accumulators(), pltpu.SMEM((1,), jnp.int32)],
        compiler_params=_params(2),
        name="sb_attn",
    )(qvt, k, qvt, tri)


def _mixout_kernel(h_ref, g_ref, oa_ref, ob_ref, oc_ref, wg_ref, pa_ref, pb_ref, pc_ref, wo_ref,
                   o_ref):
    h = h_ref[...]
    d = h.shape[1]
    u = _rms(h, g_ref[...], EPS).astype(BF16)
    m = jnp.zeros(h.shape, F32)
    for n, (o_r, p_r) in enumerate(((oa_ref, pa_ref), (ob_ref, pb_ref), (oc_ref, pc_ref))):
        gate = jax.nn.sigmoid(_dot(u, wg_ref[:, n * d:(n + 1) * d]))
        m = m + gate * _dot(o_r[...], p_r[0].astype(BF16))
    o_ref[...] = h + _dot(m.astype(BF16), wo_ref[0].astype(BF16))


def _mixout(h, g, oa, ob, oc, wg, pa, pb, pc, wo, layer):
    t, d = h.shape
    row = pl.BlockSpec((TM_DENSE, d), lambda i: (i, 0))
    orow = pl.BlockSpec((TM_DENSE, BRANCH_W), lambda i: (i, 0))
    return pl.pallas_call(
        _mixout_kernel,
        grid=(t // TM_DENSE,),
        in_specs=[row, _resident((1, d)), orow, orow, orow, _resident(wg.shape),
                  _layer_resident(pa.shape, layer), _layer_resident(pb.shape, layer),
                  _layer_resident(pc.shape, layer), _layer_resident(wo.shape, layer)],
        out_specs=row,
        out_shape=jax.ShapeDtypeStruct((t, d), F32),
        compiler_params=_params(1),
        name="mixout",
    )(h, g, oa, ob, oc, wg, pa, pb, pc, wo)


def _decay_selector():
    sel = [[0.0] * BRANCH_W for _ in range(LANES)]
    for head in range(N_FOX):
        for piece in range(N_SPLIT):
            sel[piece * BF16_ROWS + head][(head // 2) * LANES + N_SPLIT * (head % 2) + piece] = 1.0
    return jnp.asarray(sel, BF16)


def kernel(x, ffn1_norm, ffn1_w1, ffn1_w3, ffn1_w2, mix_norm, w_in, forget_bias, diff_lq1, diff_lk1,
           diff_lq2, diff_lk2, diff_subln, proj_a, proj_b, proj_c, w_out, ffn2_norm, ffn2_w1,
           ffn2_w3, ffn2_w2, final_norm):
    b, s, d = x.shape
    depth = w_in.shape[0]
    assert s % (Q_PER_STEP * T_ATTN) == 0 and (b * s) % TM_DENSE == 0 and s <= POS_RADIX * POS_RADIX
    bw = BRANCH_W
    f_lo = 6 * bw
    c_lo = f_lo + N_FOX
    g_lo = c_lo + 3 * bw

    slopes = jnp.asarray([2.0 ** (-8.0 * (hh + 1) / N_GROUPS) for hh in range(N_GROUPS)], F32)
    ridx = lax.broadcasted_iota(jnp.int32, (T_ATTN + BF16_ROWS, T_ATTN), 0)
    cidx = lax.broadcasted_iota(jnp.int32, (T_ATTN + BF16_ROWS, T_ATTN), 1)
    tri = ((cidx > ridx) | (ridx == T_ATTN)).astype(BF16)
    kpos = lax.broadcasted_iota(jnp.int32, (s, LANES), 0)
    plane = lax.broadcasted_iota(jnp.int32, (s, LANES), 1)
    pos = jnp.where(plane >= 2 * N_SPLIT, 0,
                    jnp.where(plane % 2 == 0, kpos // POS_RADIX, kpos % POS_RADIX)).astype(BF16)[None]
    sel = _decay_selector()
    row = lambda a: a.reshape(1, -1).astype(F32)

    h = x.reshape(b * s, d)
    for l in range(depth):
        h = _ffn(h, row(ffn1_norm[l]), ffn1_w1, ffn1_w3, ffn1_w2, row(final_norm), l, final=False)

        w = w_in[l]
        col = lambda lo: w[:, lo:lo + bw]
        w_k = jnp.concatenate([col(bw), col(4 * bw), col(c_lo + bw)], axis=1).astype(BF16)
        w_qvt = jnp.concatenate([col(0), col(3 * bw), col(c_lo),
                                 col(2 * bw), col(5 * bw), col(c_lo + 2 * bw)], axis=1).T.astype(BF16)
        w_ft = jnp.zeros((BF16_ROWS, d), BF16).at[:N_FOX].set(w[:, f_lo:c_lo].T.astype(BF16))
        k, qvt, ft = _inproj(h, row(mix_norm[l]), w_k, w_qvt, w_ft)
        k = k.reshape(-1, b, s, bw)

        bias = jnp.zeros((BF16_ROWS, 1), F32).at[:N_FOX, 0].set(forget_bias[l])
        kc = _decay(ft, bias, sel, b, s)

        lam_init = 0.8 - 0.6 * math.exp(-0.3 * l)
        lam_rows = jnp.zeros((8, LANES), F32).at[:4, :HEAD_DIM].set(
            jnp.stack([diff_lq1[l], diff_lk1[l], diff_lq2[l], diff_lk2[l]]))
        o_a = _diff_attn(k, qvt, pos, slopes, lam_rows, row(diff_subln[l]), lam_init)
        o_b = _fox_attn(k, qvt, kc)
        o_c = _sb_attn(k, qvt, tri)

        flat = lambda a: a.reshape(b * s, bw)
        h = _mixout(h, row(mix_norm[l]), flat(o_a), flat(o_b), flat(o_c), w[:, g_lo:].astype(BF16),
                    proj_a, proj_b, proj_c, w_out, l)

        h = _ffn(h, row(ffn2_norm[l]), ffn2_w1, ffn2_w3, ffn2_w2, row(final_norm), l,
                 final=(l == depth - 1))
    return h.reshape(b, s, d)
```

```python
import functools
import math

import jax
import jax.numpy as jnp
from jax import lax
from jax.experimental import pallas as pl
from jax.experimental.pallas import tpu as pltpu

F32 = jnp.float32
BF16 = jnp.bfloat16

EPS = 1e-6
SUBLN_EPS = 1e-5
HEAD_DIM = 64
LANES = 128
BF16_ROWS = 16
N_GROUPS = 4
N_CHAINS = 2 * N_GROUPS
BRANCH_W = N_GROUPS * LANES
N_FOX = 8
N_SPLIT = 3
QK_SCALE = 1.0 / math.sqrt(HEAD_DIM)
LOG2_E = math.log2(math.e)
Q_SCALES = (QK_SCALE * LOG2_E, QK_SCALE * LOG2_E, QK_SCALE)
POS_RADIX = 256
VMEM_LIMIT = 56 * 1024 * 1024

TM_DENSE = 512
TF_FFN = 256
TN_PROJ = 512
T_ATTN = 256
T_SUB = 128
Q_PER_STEP = 2
NEG_BIG = -1e30


def _rms(x, g, eps):
    return x * lax.rsqrt(jnp.mean(x * x, axis=-1, keepdims=True) + eps) * g


def _dot(a, b):
    return jnp.dot(a, b, preferred_element_type=F32)


def _dot_nt(a, b):
    return lax.dot_general(a, b, (((1,), (1,)), ((), ())), preferred_element_type=F32)


def _resident(shape):
    nd = len(shape)
    return pl.BlockSpec(shape, lambda *_: (0,) * nd, pipeline_mode=pl.Buffered(1))


def _layer_resident(shape, layer):
    nd = len(shape) - 1
    return pl.BlockSpec((1,) + tuple(shape[1:]), lambda *_: (layer,) + (0,) * nd,
                        pipeline_mode=pl.Buffered(1))


def _params(n_grid):
    return pltpu.CompilerParams(dimension_semantics=("arbitrary",) * n_grid,
                                vmem_limit_bytes=VMEM_LIMIT)


def _split3(x):
    hi = x.astype(BF16).astype(F32)
    r = x - hi
    mid = r.astype(BF16).astype(F32)
    return hi, mid, r - mid


def _ffn_kernel(h_ref, g_ref, w1_ref, w3_ref, w2_ref, fg_ref, o_ref, *, final):
    h = h_ref[...]
    nb = _rms(h, g_ref[...], EPS).astype(BF16)
    acc = jnp.zeros(h.shape, F32)
    for c in range(w1_ref.shape[2] // TF_FFN):
        sl = slice(c * TF_FFN, (c + 1) * TF_FFN)
        a = _dot(nb, w1_ref[0, :, sl].astype(BF16))
        b = _dot(nb, w3_ref[0, :, sl].astype(BF16))
        t = (a * jax.nn.sigmoid(a) * b).astype(BF16)
        acc = acc + _dot(t, w2_ref[0, sl, :].astype(BF16))
    out = h + 0.5 * acc
    if final:
        out = _rms(out, fg_ref[...], EPS)
    o_ref[...] = out


def _ffn(h, g, w1, w3, w2, fg, layer, final):
    t, d = h.shape
    ff = w1.shape[2]
    assert t % TM_DENSE == 0 and ff % TF_FFN == 0
    row = pl.BlockSpec((TM_DENSE, d), lambda i: (i, 0))
    return pl.pallas_call(
        functools.partial(_ffn_kernel, final=final),
        grid=(t // TM_DENSE,),
        in_specs=[row, _resident((1, d)), _layer_resident(w1.shape, layer),
                  _layer_resident(w3.shape, layer), _layer_resident(w2.shape, layer),
                  _resident((1, d))],
        out_specs=row,
        out_shape=jax.ShapeDtypeStruct((t, d), F32),
        compiler_params=_params(1),
        name="ffn",
    )(h, g, w1, w3, w2, fg)


def _inproj_kernel(h_ref, g_ref, wk_ref, wqvt_ref, wft_ref, k_ref, qvt_ref, ft_ref):
    u = _rms(h_ref[...], g_ref[...], EPS).astype(BF16)
    for c in range(wk_ref.shape[1] // TN_PROJ):
        sl = slice(c * TN_PROJ, (c + 1) * TN_PROJ)
        k_ref[c] = _dot(u, wk_ref[:, sl]).astype(BF16)
    n_chunks = wqvt_ref.shape[0] // TN_PROJ
    for c in range(n_chunks):
        sl = slice(c * TN_PROJ, (c + 1) * TN_PROJ)
        r = _dot_nt(wqvt_ref[sl, :], u)
        if c < n_chunks // 2:
            r = r * Q_SCALES[c]
        qvt_ref[sl, :] = r.astype(BF16)
    ft_ref[...] = _dot_nt(wft_ref[...], u)


def _inproj(h, g, w_k, w_qvt, w_ft):
    t, d = h.shape
    nk, nqv, nf = w_k.shape[1], w_qvt.shape[0], w_ft.shape[0]
    assert TN_PROJ == BRANCH_W and nk == 3 * BRANCH_W and nqv == 6 * BRANCH_W
    return pl.pallas_call(
        _inproj_kernel,
        grid=(t // TM_DENSE,),
        in_specs=[pl.BlockSpec((TM_DENSE, d), lambda i: (i, 0)), _resident((1, d)),
                  _resident((d, nk)), _resident((nqv, d)), _resident((nf, d))],
        out_specs=[pl.BlockSpec((nk // TN_PROJ, TM_DENSE, TN_PROJ), lambda i: (0, i, 0)),
                   pl.BlockSpec((nqv, TM_DENSE), lambda i: (0, i)),
                   pl.BlockSpec((nf, TM_DENSE), lambda i: (0, i))],
        out_shape=[jax.ShapeDtypeStruct((nk // TN_PROJ, t, TN_PROJ), BF16),
                   jax.ShapeDtypeStruct((nqv, t), BF16),
                   jax.ShapeDtypeStruct((nf, t), F32)],
        compiler_params=_params(1),
        name="inproj",
    )(h, g, w_k, w_qvt, w_ft)


def _decay_kernel(ft_ref, bias_ref, sel_ref, kc_ref):
    x = ft_ref[...] + bias_ref[...]
    x = jnp.minimum(x, 0.0) - jnp.log1p(jnp.exp(-jnp.abs(x)))
    nf, s = x.shape
    lane = lax.broadcasted_iota(jnp.int32, x.shape, 1)
    sh = 1
    while sh < s:
        x = x + jnp.where(lane >= sh, pltpu.roll(x, sh, axis=1), 0.0)
        sh *= 2
    parts = jnp.concatenate(_split3(x * LOG2_E) + (jnp.zeros((LANES - N_SPLIT * nf, s), F32),),
                            axis=0)
    kc_ref[0] = _dot(parts.T.astype(BF16), sel_ref[...]).astype(BF16)


def _decay(ft, bias, sel, b, s):
    nf = ft.shape[0]
    return pl.pallas_call(
        _decay_kernel,
        grid=(b,),
        in_specs=[pl.BlockSpec((nf, s), lambda i: (0, i)), _resident((nf, 1)),
                  _resident(sel.shape)],
        out_specs=pl.BlockSpec((1, s, BRANCH_W), lambda i: (i, 0, 0)),
        out_shape=jax.ShapeDtypeStruct((b, s, BRANCH_W), BF16),
        compiler_params=_params(1),
        name="decay",
    )(ft, bias, sel)


def _feat_iota(t):
    return lax.broadcasted_iota(jnp.int32, (LANES, t), 0)


def _group(n):
    return slice((n // 2) * LANES, (n // 2 + 1) * LANES)


def _chain_queries(qt_ref, half, n, feat=None):
    qt = qt_ref[_group(n), half * T_ATTN:(half + 1) * T_ATTN]
    sub = _feat_iota(qt.shape[1])
    lo = (n % 2) * HEAD_DIM
    q = jnp.where((sub >= lo) & (sub < lo + HEAD_DIM), qt, jnp.zeros_like(qt))
    return q if feat is None else jnp.concatenate([q, feat], axis=0)


def _keys(off, width):
    return pl.ds(pl.multiple_of(off, T_SUB), width)


def _softmax_attend(i, odd, key_operand, qs, vt_ref, s_scr, mtile_scr, mrun_scr, m_scr, l_scr,
                    acc_scr):
    t = T_ATTN
    m_scr[...] = jnp.full(m_scr.shape, NEG_BIG, F32)
    mrun_scr[...] = jnp.full(mrun_scr.shape, NEG_BIG, F32)
    l_scr[...] = jnp.zeros(l_scr.shape, F32)
    acc_scr[...] = jnp.zeros(acc_scr.shape, F32)

    def score_stage(b, slot, diagonal):
        ks = _keys(b * t, t)
        if diagonal:
            causal = (lax.broadcasted_iota(jnp.int32, (t, t), 0)
                      <= lax.broadcasted_iota(jnp.int32, (t, t), 1))
        for n in range(N_CHAINS):
            s = _dot(key_operand(ks, n), qs[n])
            if diagonal:
                s = jnp.where(causal, s, -jnp.inf)
            s_scr[slot, n] = s
            m_new = jnp.maximum(mrun_scr[n], jnp.max(s, axis=0, keepdims=True))
            mrun_scr[n] = m_new
            mtile_scr[slot, n] = m_new

    def value_stage(b, slot):
        ks = _keys(b * t, t)
        for n in range(N_CHAINS):
            m_new = mtile_scr[slot, n]
            p = jnp.exp2(s_scr[slot, n] - m_new)
            alpha = jnp.exp2(m_scr[n] - m_new)
            l_scr[n] = alpha * l_scr[n] + jnp.sum(p, axis=0, keepdims=True)
            acc_scr[n] = alpha * acc_scr[n] + _dot(vt_ref[_group(n), ks], p.astype(BF16))
            m_scr[n] = m_new

    def two_tiles(p, _):
        score_stage(2 * p + 1, 1, False)
        value_stage(2 * p, 0)
        score_stage(2 * p + 2, 0, False)
        value_stage(2 * p + 1, 1)
        return 0

    if odd:
        score_stage(0, 0, False)
        lax.fori_loop(0, (i - 1) // 2, two_tiles, 0)
        score_stage(i, 1, True)
        value_stage(i - 1, 0)
        value_stage(i, 1)
    else:
        @pl.when(i == 0)
        def _():
            score_stage(0, 0, True)
            value_stage(0, 0)

        @pl.when(i > 0)
        def _():
            score_stage(0, 0, False)
            lax.fori_loop(0, (i - 1) // 2, two_tiles, 0)
            score_stage(i - 1, 1, False)
            value_stage(i - 2, 0)
            score_stage(i, 0, True)
            value_stage(i - 1, 1)
            value_stage(i, 0)

    return [acc_scr[n] * (1.0 / l_scr[n]) for n in range(N_CHAINS)]


def _chain_rows():
    return pltpu.VMEM((N_CHAINS, 1, T_ATTN), F32)


def _chain_accumulators():
    return pltpu.VMEM((N_CHAINS, LANES, T_ATTN), F32)


def _softmax_scratch():
    return [pltpu.VMEM((2, N_CHAINS, T_ATTN, T_ATTN), F32), pltpu.VMEM((2, N_CHAINS, 1, T_ATTN), F32),
            _chain_rows(), _chain_rows(), _chain_rows(), _chain_accumulators()]


def _head_rows(a0, a1):
    sub = lax.broadcasted_iota(jnp.int32, a0.shape, 0)
    return jnp.where(sub < HEAD_DIM, a0, a1)


def _attn_specs(s, branch):
    tq = Q_PER_STEP * T_ATTN
    steps = s // tq
    qt_spec = pl.BlockSpec((BRANCH_W, tq), lambda b, r: (branch, b * steps + r))
    k_spec = pl.BlockSpec((1, 1, s, BRANCH_W), lambda b, r: (branch, b, 0, 0))
    vt_spec = pl.BlockSpec((BRANCH_W, s), lambda b, r: (3 + branch, b))
    o_spec = pl.BlockSpec((1, tq, BRANCH_W), lambda b, r: (b, r, 0))
    return qt_spec, k_spec, vt_spec, o_spec


def _query_tiles():
    r = pl.program_id(1)
    return [(half, Q_PER_STEP * r + half, half % 2 == 1) for half in range(Q_PER_STEP)]


def _out_tile(o_ref, half, g):
    return o_ref.at[0, half * T_ATTN:(half + 1) * T_ATTN, g * LANES:(g + 1) * LANES]


def _diff_kernel(slopes_ref, qt_ref, k_ref, pos_ref, vt_ref, lam_ref, subln_ref, o_ref,
                 *softmax_scratch, lam_init):
    sub = _feat_iota(T_ATTN)
    feats = []
    for g in range(N_GROUPS):
        pieces = _split3(jnp.full(sub.shape, slopes_ref[g], F32) * LOG2_E)
        feat = jnp.zeros(sub.shape, F32)
        for j, piece in enumerate(pieces):
            feat = jnp.where(sub == 2 * j, piece * POS_RADIX, jnp.where(sub == 2 * j + 1, piece, feat))
        feats.append(feat.astype(BF16))

    def key_operand(ks, n):
        return jnp.concatenate([k_ref[0, 0, ks, _group(n)], pos_ref[0, ks, :]], axis=1)

    lv = lam_ref[...]
    lam = (jnp.exp(jnp.sum(lv[0:1] * lv[1:2], axis=-1, keepdims=True))
           - jnp.exp(jnp.sum(lv[2:3] * lv[3:4], axis=-1, keepdims=True)) + lam_init)
    for half, i, odd in _query_tiles():
        qs = [_chain_queries(qt_ref, half, n, feats[n // 2]) for n in range(N_CHAINS)]
        acc = _softmax_attend(i, odd, key_operand, qs, vt_ref, *softmax_scratch)
        for g in range(N_GROUPS):
            o = (acc[2 * g] - lam * acc[2 * g + 1]).T
            o = _rms(o, subln_ref[...], SUBLN_EPS) * (1.0 - lam_init)
            _out_tile(o_ref, half, g)[...] = o.astype(o_ref.dtype)


def _diff_attn(k, qvt, pos, slopes, lam_rows, subln, lam_init):
    _, b, s, _ = k.shape
    qt_spec, k_spec, vt_spec, o_spec = _attn_specs(s, 0)
    return pl.pallas_call(
        functools.partial(_diff_kernel, lam_init=lam_init),
        grid=(b, s // (Q_PER_STEP * T_ATTN)),
        in_specs=[pl.BlockSpec(memory_space=pltpu.SMEM), qt_spec, k_spec, _resident(pos.shape),
                  vt_spec, _resident(lam_rows.shape), _resident(subln.shape)],
        out_specs=o_spec,
        out_shape=jax.ShapeDtypeStruct((b, s, BRANCH_W), BF16),
        scratch_shapes=_softmax_scratch(),
        compiler_params=_params(2),
        name="diff_attn",
    )(slopes, qvt, k, pos, qvt, lam_rows, subln)


def _fox_kernel(qt_ref, k_ref, kc_ref, vt_ref, o_ref, *softmax_scratch):
    sub = _feat_iota(T_ATTN)
    feats = [jnp.where((sub >= N_SPLIT * c) & (sub < N_SPLIT * (c + 1)), -1.0, 0.0).astype(BF16)
             for c in range(2)]

    def key_operand(ks, n):
        return jnp.concatenate([k_ref[0, 0, ks, _group(n)], kc_ref[0, ks, _group(n)]], axis=1)

    for half, i, odd in _query_tiles():
        qs = [_chain_queries(qt_ref, half, n, feats[n % 2]) for n in range(N_CHAINS)]
        acc = _softmax_attend(i, odd, key_operand, qs, vt_ref, *softmax_scratch)
        for g in range(N_GROUPS):
            _out_tile(o_ref, half, g)[...] = _head_rows(
                acc[2 * g], acc[2 * g + 1]).T.astype(o_ref.dtype)


def _fox_attn(k, qvt, kc):
    _, b, s, _ = k.shape
    qt_spec, k_spec, vt_spec, o_spec = _attn_specs(s, 1)
    kc_spec = pl.BlockSpec((1, s, BRANCH_W), lambda b_, r: (b_, 0, 0))
    return pl.pallas_call(
        _fox_kernel,
        grid=(b, s // (Q_PER_STEP * T_ATTN)),
        in_specs=[qt_spec, k_spec, kc_spec, vt_spec],
        out_specs=o_spec,
        out_shape=jax.ShapeDtypeStruct((b, s, BRANCH_W), BF16),
        scratch_shapes=_softmax_scratch(),
        compiler_params=_params(2),
        name="fox_attn",
    )(qvt, k, kc, qvt)


def _sb_kernel(qt_ref, k_ref, vt_ref, tri_ref, o_ref, run_scr, acc_scr, alive_ref):
    tri = tri_ref[...]
    for half, i, odd in _query_tiles():
        qs = [_chain_queries(qt_ref, half, n) for n in range(N_CHAINS)]
        _sb_attend(i, odd, qs, tri, k_ref, vt_ref, run_scr, acc_scr, alive_ref)
        for g in range(N_GROUPS):
            _out_tile(o_ref, half, g)[...] = _head_rows(
                acc_scr[2 * g], acc_scr[2 * g + 1]).T.astype(o_ref.dtype)


def _sb_attend(i, odd, qs, tri, k_ref, vt_ref, run_scr, acc_scr, alive_ref):
    t = T_ATTN
    run_scr[...] = jnp.zeros(run_scr.shape, F32)
    acc_scr[...] = jnp.zeros(acc_scr.shape, F32)

    def block(off, width, diag_from=None):
        zs = [_dot(k_ref[0, 0, _keys(off, width), _group(n)], qs[n]) for n in range(N_CHAINS)]
        n_sub = width // T_SUB
        masks = {}
        if diag_from is not None:
            key = lax.broadcasted_iota(jnp.int32, (T_SUB, t), 0)
            qry = lax.broadcasted_iota(jnp.int32, (T_SUB, t), 1)
            for u in range(n_sub):
                if u * T_SUB >= diag_from:
                    masks[u] = key + (u * T_SUB - diag_from) < qry
        mid = []
        for n in range(N_CHAINS):
            subs = []
            for u in range(n_sub):
                z = zs[n][u * T_SUB:(u + 1) * T_SUB]
                log_beta = jnp.minimum(z, 0.0) - jnp.log(1.0 + jnp.exp(jnp.minimum(z, -z)))
                log_keep = log_beta - z
                if u in masks:
                    log_keep = jnp.where(masks[u], log_keep, 0.0)
                hi = log_keep.astype(BF16)
                lo = (log_keep - hi.astype(F32)).astype(BF16)
                cs = _dot(tri, hi) + _dot(tri, lo)
                subs.append((log_beta + cs[:T_SUB], cs[T_SUB:T_SUB + 1]))
            mid.append(subs)
        for n in range(N_CHAINS):
            run = run_scr[n]
            acc = acc_scr[n]
            for u in reversed(range(n_sub)):
                w = jnp.exp(mid[n][u][0])
                if u in masks:
                    w = jnp.where(masks[u], w, 0.0)
                acc = acc + jnp.exp(run) * _dot(vt_ref[_group(n), _keys(off + u * T_SUB, T_SUB)],
                                                w.astype(BF16))
                run = run + mid[n][u][1]
            run_scr[n] = run
            acc_scr[n] = acc
            top = run if n == 0 else jnp.maximum(top, run)
        alive_ref[0] = (jnp.max(jnp.exp(top)) > 0.0).astype(jnp.int32)

    alive = lambda: alive_ref[0] > 0

    def first_blocks():
        block(i * t - T_SUB, t + T_SUB, diag_from=T_SUB)
        pl.when(alive())(lambda: block((i - 1) * t, t - T_SUB))

    if odd:
        first_blocks()
    else:
        pl.when(i == 0)(lambda: block(0, t, diag_from=0))
        pl.when(i > 0)(first_blocks)

    def two_tiles(p, _):
        pl.when(alive())(lambda: block((i - 3 - 2 * p) * t, 2 * t))
        return 0

    lax.fori_loop(0, (i - 1) // 2, two_tiles, 0)
    pl.when((i >= 2) & (i % 2 == 0) & alive())(lambda: block(0, t))


def _sb_attn(k, qvt, tri):
    _, b, s, _ = k.shape
    qt_spec, k_spec, vt_spec, o_spec = _attn_specs(s, 2)
    return pl.pallas_call(
        _sb_kernel,
        grid=(b, s // (Q_PER_STEP * T_ATTN)),
        in_specs=[qt_spec, k_spec, vt_spec, _resident(tri.shape)],
        out_specs=o_spec,
        out_shape=jax.ShapeDtypeStruct((b, s, BRANCH_W), BF16),
        scratch_shapes=[_chain_rows(), _chain_accumulators(), pltpu.SMEM((1,), jnp.int32)],
        compiler_params=_params(2),
        name="sb_attn",
    )(qvt, k, qvt, tri)


def _mixout_kernel(h_ref, g_ref, oa_ref, ob_ref, oc_ref, wg_ref, pa_ref, pb_ref, pc_ref, wo_ref,
                   o_ref):
    h = h_ref[...]
    d = h.shape[1]
    u = _rms(h, g_ref[...], EPS).astype(BF16)
    m = jnp.zeros(h.shape, F32)
    for n, (o_r, p_r) in enumerate(((oa_ref, pa_ref), (ob_ref, pb_ref), (oc_ref, pc_ref))):
        gate = jax.nn.sigmoid(_dot(u, wg_ref[:, n * d:(n + 1) * d]))
        m = m + gate * _dot(o_r[...], p_r[0].astype(BF16))
    o_ref[...] = h + _dot(m.astype(BF16), wo_ref[0].astype(BF16))


def _mixout(h, g, oa, ob, oc, wg, pa, pb, pc, wo, layer):
    t, d = h.shape
    row = pl.BlockSpec((TM_DENSE, d), lambda i: (i, 0))
    orow = pl.BlockSpec((TM_DENSE, BRANCH_W), lambda i: (i, 0))
    return pl.pallas_call(
        _mixout_kernel,
        grid=(t // TM_DENSE,),
        in_specs=[row, _resident((1, d)), orow, orow, orow, _resident(wg.shape),
                  _layer_resident(pa.shape, layer), _layer_resident(pb.shape, layer),
                  _layer_resident(pc.shape, layer), _layer_resident(wo.shape, layer)],
        out_specs=row,
        out_shape=jax.ShapeDtypeStruct((t, d), F32),
        compiler_params=_params(1),
        name="mixout",
    )(h, g, oa, ob, oc, wg, pa, pb, pc, wo)


def _decay_selector():
    sel = [[0.0] * BRANCH_W for _ in range(LANES)]
    for head in range(N_FOX):
        for piece in range(N_SPLIT):
            sel[piece * BF16_ROWS + head][(head // 2) * LANES + N_SPLIT * (head % 2) + piece] = 1.0
    return jnp.asarray(sel, BF16)


def kernel(x, ffn1_norm, ffn1_w1, ffn1_w3, ffn1_w2, mix_norm, w_in, forget_bias, diff_lq1, diff_lk1,
           diff_lq2, diff_lk2, diff_subln, proj_a, proj_b, proj_c, w_out, ffn2_norm, ffn2_w1,
           ffn2_w3, ffn2_w2, final_norm):
    b, s, d = x.shape
    depth = w_in.shape[0]
    assert s % (Q_PER_STEP * T_ATTN) == 0 and (b * s) % TM_DENSE == 0 and s <= POS_RADIX * POS_RADIX
    bw = BRANCH_W
    f_lo = 6 * bw
    c_lo = f_lo + N_FOX
    g_lo = c_lo + 3 * bw

    slopes = jnp.asarray([2.0 ** (-8.0 * (hh + 1) / N_GROUPS) for hh in range(N_GROUPS)], F32)
    ridx = lax.broadcasted_iota(jnp.int32, (T_SUB + BF16_ROWS, T_SUB), 0)
    cidx = lax.broadcasted_iota(jnp.int32, (T_SUB + BF16_ROWS, T_SUB), 1)
    tri = ((cidx > ridx) | (ridx == T_SUB)).astype(BF16)
    kpos = lax.broadcasted_iota(jnp.int32, (s, LANES), 0)
    plane = lax.broadcasted_iota(jnp.int32, (s, LANES), 1)
    pos = jnp.where(plane >= 2 * N_SPLIT, 0,
                    jnp.where(plane % 2 == 0, kpos // POS_RADIX, kpos % POS_RADIX)).astype(BF16)[None]
    sel = _decay_selector()
    row = lambda a: a.reshape(1, -1).astype(F32)

    h = x.reshape(b * s, d)
    for l in range(depth):
        h = _ffn(h, row(ffn1_norm[l]), ffn1_w1, ffn1_w3, ffn1_w2, row(final_norm), l, final=False)

        w = w_in[l]
        col = lambda lo: w[:, lo:lo + bw]
        w_k = jnp.concatenate([col(bw), col(4 * bw), col(c_lo + bw)], axis=1).astype(BF16)
        w_qvt = jnp.concatenate([col(0), col(3 * bw), col(c_lo),
                                 col(2 * bw), col(5 * bw), col(c_lo + 2 * bw)], axis=1).T.astype(BF16)
        w_ft = jnp.zeros((BF16_ROWS, d), BF16).at[:N_FOX].set(w[:, f_lo:c_lo].T.astype(BF16))
        k, qvt, ft = _inproj(h, row(mix_norm[l]), w_k, w_qvt, w_ft)
        k = k.reshape(-1, b, s, bw)

        bias = jnp.zeros((BF16_ROWS, 1), F32).at[:N_FOX, 0].set(forget_bias[l])
        kc = _decay(ft, bias, sel, b, s)

        lam_init = 0.8 - 0.6 * math.exp(-0.3 * l)
        lam_rows = jnp.zeros((8, LANES), F32).at[:4, :HEAD_DIM].set(
            jnp.stack([diff_lq1[l], diff_lk1[l], diff_lq2[l], diff_lk2[l]]))
        o_a = _diff_attn(k, qvt, pos, slopes, lam_rows, row(diff_subln[l]), lam_init)
        o_b = _fox_attn(k, qvt, kc)
        o_c = _sb_attn(k, qvt, tri)

        flat = lambda a: a.reshape(b * s, bw)
        h = _mixout(h, row(mix_norm[l]), flat(o_a), flat(o_b), flat(o_c), w[:, g_lo:].astype(BF16),
                    proj_a, proj_b, proj_c, w_out, l)

        h = _ffn(h, row(ffn2_norm[l]), ffn2_w1, ffn2_w3, ffn2_w2, row(final_norm), l,
                 final=(l == depth - 1))
    return h.reshape(b, s, d)
```

```python
import functools
import math

import jax
import jax.numpy as jnp
from jax import lax
from jax.experimental import pallas as pl
from jax.experimental.pallas import tpu as pltpu

F32 = jnp.float32
BF16 = jnp.bfloat16

EPS = 1e-6
SUBLN_EPS = 1e-5
HEAD_DIM = 64
LANES = 128
BF16_ROWS = 16
N_GROUPS = 4
N_CHAINS = 2 * N_GROUPS
BRANCH_W = N_GROUPS * LANES
N_FOX = 8
N_SPLIT = 3
QK_SCALE = 1.0 / math.sqrt(HEAD_DIM)
LOG2_E = math.log2(math.e)
Q_SCALES = (QK_SCALE * LOG2_E, QK_SCALE * LOG2_E, QK_SCALE)
POS_RADIX = 256
VMEM_LIMIT = 56 * 1024 * 1024

TM_DENSE = 512
TF_FFN = 256
TN_PROJ = 512
T_ATTN = 256
Q_PER_STEP = 4
NEG_BIG = -1e30


def _rms(x, g, eps):
    return x * lax.rsqrt(jnp.mean(x * x, axis=-1, keepdims=True) + eps) * g


def _dot(a, b):
    return jnp.dot(a, b, preferred_element_type=F32)


def _dot_nt(a, b):
    return lax.dot_general(a, b, (((1,), (1,)), ((), ())), preferred_element_type=F32)


def _resident(shape):
    nd = len(shape)
    return pl.BlockSpec(shape, lambda *_: (0,) * nd, pipeline_mode=pl.Buffered(1))


def _layer_resident(shape, layer):
    nd = len(shape) - 1
    return pl.BlockSpec((1,) + tuple(shape[1:]), lambda *_: (layer,) + (0,) * nd,
                        pipeline_mode=pl.Buffered(1))


def _params(n_grid):
    return pltpu.CompilerParams(dimension_semantics=("arbitrary",) * n_grid,
                                vmem_limit_bytes=VMEM_LIMIT)


def _split3(x):
    hi = x.astype(BF16).astype(F32)
    r = x - hi
    mid = r.astype(BF16).astype(F32)
    return hi, mid, r - mid


def _ffn_kernel(h_ref, g_ref, w1_ref, w3_ref, w2_ref, fg_ref, o_ref, *, final):
    h = h_ref[...]
    nb = _rms(h, g_ref[...], EPS).astype(BF16)
    acc = jnp.zeros(h.shape, F32)
    for c in range(w1_ref.shape[2] // TF_FFN):
        sl = slice(c * TF_FFN, (c + 1) * TF_FFN)
        a = _dot(nb, w1_ref[0, :, sl].astype(BF16))
        b = _dot(nb, w3_ref[0, :, sl].astype(BF16))
        t = (a * jax.nn.sigmoid(a) * b).astype(BF16)
        acc = acc + _dot(t, w2_ref[0, sl, :].astype(BF16))
    out = h + 0.5 * acc
    if final:
        out = _rms(out, fg_ref[...], EPS)
    o_ref[...] = out


def _ffn(h, g, w1, w3, w2, fg, layer, final):
    t, d = h.shape
    ff = w1.shape[2]
    assert t % TM_DENSE == 0 and ff % TF_FFN == 0
    row = pl.BlockSpec((TM_DENSE, d), lambda i: (i, 0))
    return pl.pallas_call(
        functools.partial(_ffn_kernel, final=final),
        grid=(t // TM_DENSE,),
        in_specs=[row, _resident((1, d)), _layer_resident(w1.shape, layer),
                  _layer_resident(w3.shape, layer), _layer_resident(w2.shape, layer),
                  _resident((1, d))],
        out_specs=row,
        out_shape=jax.ShapeDtypeStruct((t, d), F32),
        compiler_params=_params(1),
        name="ffn",
    )(h, g, w1, w3, w2, fg)


def _inproj_kernel(h_ref, g_ref, wk_ref, wqvt_ref, wft_ref, k_ref, qvt_ref, ft_ref):
    u = _rms(h_ref[...], g_ref[...], EPS).astype(BF16)
    for c in range(wk_ref.shape[1] // TN_PROJ):
        sl = slice(c * TN_PROJ, (c + 1) * TN_PROJ)
        k_ref[c] = _dot(u, wk_ref[:, sl]).astype(BF16)
    n_chunks = wqvt_ref.shape[0] // TN_PROJ
    for c in range(n_chunks):
        sl = slice(c * TN_PROJ, (c + 1) * TN_PROJ)
        r = _dot_nt(wqvt_ref[sl, :], u)
        if c < n_chunks // 2:
            r = r * Q_SCALES[c]
        qvt_ref[sl, :] = r.astype(BF16)
    ft_ref[...] = _dot_nt(wft_ref[...], u)


def _inproj(h, g, w_k, w_qvt, w_ft):
    t, d = h.shape
    nk, nqv, nf = w_k.shape[1], w_qvt.shape[0], w_ft.shape[0]
    assert TN_PROJ == BRANCH_W and nk == 3 * BRANCH_W and nqv == 6 * BRANCH_W
    return pl.pallas_call(
        _inproj_kernel,
        grid=(t // TM_DENSE,),
        in_specs=[pl.BlockSpec((TM_DENSE, d), lambda i: (i, 0)), _resident((1, d)),
                  _resident((d, nk)), _resident((nqv, d)), _resident((nf, d))],
        out_specs=[pl.BlockSpec((nk // TN_PROJ, TM_DENSE, TN_PROJ), lambda i: (0, i, 0)),
                   pl.BlockSpec((nqv, TM_DENSE), lambda i: (0, i)),
                   pl.BlockSpec((nf, TM_DENSE), lambda i: (0, i))],
        out_shape=[jax.ShapeDtypeStruct((nk // TN_PROJ, t, TN_PROJ), BF16),
                   jax.ShapeDtypeStruct((nqv, t), BF16),
                   jax.ShapeDtypeStruct((nf, t), F32)],
        compiler_params=_params(1),
        name="inproj",
    )(h, g, w_k, w_qvt, w_ft)


def _decay_kernel(ft_ref, bias_ref, sel_ref, kc_ref):
    x = ft_ref[...] + bias_ref[...]
    x = jnp.minimum(x, 0.0) - jnp.log1p(jnp.exp(-jnp.abs(x)))
    nf, s = x.shape
    lane = lax.broadcasted_iota(jnp.int32, x.shape, 1)
    sh = 1
    while sh < s:
        x = x + jnp.where(lane >= sh, pltpu.roll(x, sh, axis=1), 0.0)
        sh *= 2
    parts = jnp.concatenate(_split3(x * LOG2_E) + (jnp.zeros((LANES - N_SPLIT * nf, s), F32),),
                            axis=0)
    kc_ref[0] = _dot(parts.T.astype(BF16), sel_ref[...]).astype(BF16)


def _decay(ft, bias, sel, b, s):
    nf = ft.shape[0]
    return pl.pallas_call(
        _decay_kernel,
        grid=(b,),
        in_specs=[pl.BlockSpec((nf, s), lambda i: (0, i)), _resident((nf, 1)),
                  _resident(sel.shape)],
        out_specs=pl.BlockSpec((1, s, BRANCH_W), lambda i: (i, 0, 0)),
        out_shape=jax.ShapeDtypeStruct((b, s, BRANCH_W), BF16),
        compiler_params=_params(1),
        name="decay",
    )(ft, bias, sel)


def _feat_iota(t):
    return lax.broadcasted_iota(jnp.int32, (LANES, t), 0)


def _group(n):
    return slice((n // 2) * LANES, (n // 2 + 1) * LANES)


def _chain_queries(qt_ref, half, n, feat=None):
    qt = qt_ref[_group(n), half * T_ATTN:(half + 1) * T_ATTN]
    sub = _feat_iota(qt.shape[1])
    lo = (n % 2) * HEAD_DIM
    q = jnp.where((sub >= lo) & (sub < lo + HEAD_DIM), qt, jnp.zeros_like(qt))
    return q if feat is None else jnp.concatenate([q, feat], axis=0)


def _keys(off, width):
    return pl.ds(pl.multiple_of(off, T_ATTN), width)


def _softmax_attend(i, odd, key_operand, qs, vt_ref, s_scr, mtile_scr, mrun_scr, m_scr, l_scr,
                    acc_scr):
    t = T_ATTN
    m_scr[...] = jnp.full(m_scr.shape, NEG_BIG, F32)
    mrun_scr[...] = jnp.full(mrun_scr.shape, NEG_BIG, F32)
    l_scr[...] = jnp.zeros(l_scr.shape, F32)
    acc_scr[...] = jnp.zeros(acc_scr.shape, F32)

    def score_stage(b, slot, diagonal):
        ks = _keys(b * t, t)
        if diagonal:
            causal = (lax.broadcasted_iota(jnp.int32, (t, t), 0)
                      <= lax.broadcasted_iota(jnp.int32, (t, t), 1))
        for n in range(N_CHAINS):
            s = _dot(key_operand(ks, n), qs[n])
            if diagonal:
                s = jnp.where(causal, s, -jnp.inf)
            s_scr[slot, n] = s
            m_new = jnp.maximum(mrun_scr[n], jnp.max(s, axis=0, keepdims=True))
            mrun_scr[n] = m_new
            mtile_scr[slot, n] = m_new

    def value_stage(b, slot):
        ks = _keys(b * t, t)
        for n in range(N_CHAINS):
            m_new = mtile_scr[slot, n]
            p = jnp.exp2(s_scr[slot, n] - m_new)
            alpha = jnp.exp2(m_scr[n] - m_new)
            l_scr[n] = alpha * l_scr[n] + jnp.sum(p, axis=0, keepdims=True)
            acc_scr[n] = alpha * acc_scr[n] + _dot(vt_ref[_group(n), ks], p.astype(BF16))
            m_scr[n] = m_new

    def two_tiles(p, _):
        score_stage(2 * p + 1, 1, False)
        value_stage(2 * p, 0)
        score_stage(2 * p + 2, 0, False)
        value_stage(2 * p + 1, 1)
        return 0

    if odd:
        score_stage(0, 0, False)
        lax.fori_loop(0, (i - 1) // 2, two_tiles, 0)
        score_stage(i, 1, True)
        value_stage(i - 1, 0)
        value_stage(i, 1)
    else:
        @pl.when(i == 0)
        def _():
            score_stage(0, 0, True)
            value_stage(0, 0)

        @pl.when(i > 0)
        def _():
            score_stage(0, 0, False)
            lax.fori_loop(0, (i - 1) // 2, two_tiles, 0)
            score_stage(i - 1, 1, False)
            value_stage(i - 2, 0)
            score_stage(i, 0, True)
            value_stage(i - 1, 1)
            value_stage(i, 0)

    return [acc_scr[n] * (1.0 / l_scr[n]) for n in range(N_CHAINS)]


def _chain_rows():
    return pltpu.VMEM((N_CHAINS, 1, T_ATTN), F32)


def _chain_accumulators():
    return pltpu.VMEM((N_CHAINS, LANES, T_ATTN), F32)


def _softmax_scratch():
    return [pltpu.VMEM((2, N_CHAINS, T_ATTN, T_ATTN), F32), pltpu.VMEM((2, N_CHAINS, 1, T_ATTN), F32),
            _chain_rows(), _chain_rows(), _chain_rows(), _chain_accumulators()]


def _head_rows(a0, a1):
    sub = lax.broadcasted_iota(jnp.int32, a0.shape, 0)
    return jnp.where(sub < HEAD_DIM, a0, a1)


def _attn_specs(s, branch):
    tq = Q_PER_STEP * T_ATTN
    steps = s // tq
    qt_spec = pl.BlockSpec((BRANCH_W, tq), lambda b, r: (branch, b * steps + r))
    k_spec = pl.BlockSpec((1, 1, s, BRANCH_W), lambda b, r: (branch, b, 0, 0))
    vt_spec = pl.BlockSpec((BRANCH_W, s), lambda b, r: (3 + branch, b))
    o_spec = pl.BlockSpec((1, tq, BRANCH_W), lambda b, r: (b, r, 0))
    return qt_spec, k_spec, vt_spec, o_spec


def _query_tiles():
    r = pl.program_id(1)
    return [(half, Q_PER_STEP * r + half, half % 2 == 1) for half in range(Q_PER_STEP)]


def _out_tile(o_ref, half, g):
    return o_ref.at[0, half * T_ATTN:(half + 1) * T_ATTN, g * LANES:(g + 1) * LANES]


def _diff_kernel(slopes_ref, qt_ref, k_ref, pos_ref, vt_ref, lam_ref, subln_ref, o_ref,
                 *softmax_scratch, lam_init):
    sub = _feat_iota(T_ATTN)
    feats = []
    for g in range(N_GROUPS):
        pieces = _split3(jnp.full(sub.shape, slopes_ref[g], F32) * LOG2_E)
        feat = jnp.zeros(sub.shape, F32)
        for j, piece in enumerate(pieces):
            feat = jnp.where(sub == 2 * j, piece * POS_RADIX, jnp.where(sub == 2 * j + 1, piece, feat))
        feats.append(feat.astype(BF16))

    def key_operand(ks, n):
        return jnp.concatenate([k_ref[0, 0, ks, _group(n)], pos_ref[0, ks, :]], axis=1)

    lv = lam_ref[...]
    lam = (jnp.exp(jnp.sum(lv[0:1] * lv[1:2], axis=-1, keepdims=True))
           - jnp.exp(jnp.sum(lv[2:3] * lv[3:4], axis=-1, keepdims=True)) + lam_init)
    for half, i, odd in _query_tiles():
        qs = [_chain_queries(qt_ref, half, n, feats[n // 2]) for n in range(N_CHAINS)]
        acc = _softmax_attend(i, odd, key_operand, qs, vt_ref, *softmax_scratch)
        for g in range(N_GROUPS):
            o = (acc[2 * g] - lam * acc[2 * g + 1]).T
            o = _rms(o, subln_ref[...], SUBLN_EPS) * (1.0 - lam_init)
            _out_tile(o_ref, half, g)[...] = o.astype(o_ref.dtype)


def _diff_attn(k, qvt, pos, slopes, lam_rows, subln, lam_init):
    _, b, s, _ = k.shape
    qt_spec, k_spec, vt_spec, o_spec = _attn_specs(s, 0)
    return pl.pallas_call(
        functools.partial(_diff_kernel, lam_init=lam_init),
        grid=(b, s // (Q_PER_STEP * T_ATTN)),
        in_specs=[pl.BlockSpec(memory_space=pltpu.SMEM), qt_spec, k_spec, _resident(pos.shape),
                  vt_spec, _resident(lam_rows.shape), _resident(subln.shape)],
        out_specs=o_spec,
        out_shape=jax.ShapeDtypeStruct((b, s, BRANCH_W), BF16),
        scratch_shapes=_softmax_scratch(),
        compiler_params=_params(2),
        name="diff_attn",
    )(slopes, qvt, k, pos, qvt, lam_rows, subln)


def _fox_kernel(qt_ref, k_ref, kc_ref, vt_ref, o_ref, *softmax_scratch):
    sub = _feat_iota(T_ATTN)
    feats = [jnp.where((sub >= N_SPLIT * c) & (sub < N_SPLIT * (c + 1)), -1.0, 0.0).astype(BF16)
             for c in range(2)]

    def key_operand(ks, n):
        return jnp.concatenate([k_ref[0, 0, ks, _group(n)], kc_ref[0, ks, _group(n)]], axis=1)

    for half, i, odd in _query_tiles():
        qs = [_chain_queries(qt_ref, half, n, feats[n % 2]) for n in range(N_CHAINS)]
        acc = _softmax_attend(i, odd, key_operand, qs, vt_ref, *softmax_scratch)
        for g in range(N_GROUPS):
            _out_tile(o_ref, half, g)[...] = _head_rows(
                acc[2 * g], acc[2 * g + 1]).T.astype(o_ref.dtype)


def _fox_attn(k, qvt, kc):
    _, b, s, _ = k.shape
    qt_spec, k_spec, vt_spec, o_spec = _attn_specs(s, 1)
    kc_spec = pl.BlockSpec((1, s, BRANCH_W), lambda b_, r: (b_, 0, 0))
    return pl.pallas_call(
        _fox_kernel,
        grid=(b, s // (Q_PER_STEP * T_ATTN)),
        in_specs=[qt_spec, k_spec, kc_spec, vt_spec],
        out_specs=o_spec,
        out_shape=jax.ShapeDtypeStruct((b, s, BRANCH_W), BF16),
        scratch_shapes=_softmax_scratch(),
        compiler_params=_params(2),
        name="fox_attn",
    )(qvt, k, kc, qvt)


def _sb_kernel(qt_ref, k_ref, vt_ref, tri_ref, o_ref, run_scr, acc_scr, alive_ref):
    tri = tri_ref[...]
    for half, i, odd in _query_tiles():
        qs = [_chain_queries(qt_ref, half, n) for n in range(N_CHAINS)]
        _sb_attend(i, odd, qs, tri, k_ref, vt_ref, run_scr, acc_scr, alive_ref)
        for g in range(N_GROUPS):
            _out_tile(o_ref, half, g)[...] = _head_rows(
                acc_scr[2 * g], acc_scr[2 * g + 1]).T.astype(o_ref.dtype)


def _sb_attend(i, odd, qs, tri, k_ref, vt_ref, run_scr, acc_scr, alive_ref):
    t = T_ATTN
    run_scr[...] = jnp.zeros(run_scr.shape, F32)
    acc_scr[...] = jnp.zeros(acc_scr.shape, F32)

    def block(off, width, diagonal):
        zs = [_dot(k_ref[0, 0, _keys(off, width), _group(n)], qs[n]) for n in range(N_CHAINS)]
        if diagonal:
            strict = (lax.broadcasted_iota(jnp.int32, (t, t), 0)
                      < lax.broadcasted_iota(jnp.int32, (t, t), 1))
        n_tiles = width // t
        mid = []
        for n in range(N_CHAINS):
            tiles = []
            for u in range(n_tiles):
                z = zs[n][u * t:(u + 1) * t]
                log_beta = jnp.minimum(z, 0.0) - jnp.log(1.0 + jnp.exp(jnp.minimum(z, -z)))
                log_keep = log_beta - z
                if diagonal and u == n_tiles - 1:
                    log_keep = jnp.where(strict, log_keep, 0.0)
                hi = log_keep.astype(BF16)
                lo = (log_keep - hi.astype(F32)).astype(BF16)
                cs = _dot(tri, hi) + _dot(tri, lo)
                tiles.append((log_beta + cs[:t], cs[t:t + 1]))
            mid.append(tiles)
        for n in range(N_CHAINS):
            run = run_scr[n]
            acc = acc_scr[n]
            for u in reversed(range(n_tiles)):
                w = jnp.exp(mid[n][u][0])
                if diagonal and u == n_tiles - 1:
                    w = jnp.where(strict, w, 0.0)
                acc = acc + jnp.exp(run) * _dot(vt_ref[_group(n), _keys(off + u * t, t)],
                                                w.astype(BF16))
                run = run + mid[n][u][1]
            run_scr[n] = run
            acc_scr[n] = acc
            top = run if n == 0 else jnp.maximum(top, run)
        alive_ref[0] = (jnp.max(jnp.exp(top)) > 0.0).astype(jnp.int32)

    alive = lambda: alive_ref[0] > 0
    if odd:
        block((i - 1) * t, 2 * t, True)
    else:
        pl.when(i == 0)(lambda: block(0, t, True))
        pl.when(i > 0)(lambda: block((i - 1) * t, 2 * t, True))

    def two_tiles(p, _):
        pl.when(alive())(lambda: block((i - 3 - 2 * p) * t, 2 * t, False))
        return 0

    lax.fori_loop(0, (i - 1) // 2, two_tiles, 0)
    pl.when((i >= 2) & (i % 2 == 0) & alive())(lambda: block(0, t, False))


def _sb_attn(k, qvt, tri):
    _, b, s, _ = k.shape
    qt_spec, k_spec, vt_spec, o_spec = _attn_specs(s, 2)
    return pl.pallas_call(
        _sb_kernel,
        grid=(b, s // (Q_PER_STEP * T_ATTN)),
        in_specs=[qt_spec, k_spec, vt_spec, _resident(tri.shape)],
        out_specs=o_spec,
        out_shape=jax.ShapeDtypeStruct((b, s, BRANCH_W), BF16),
        scratch_shapes=[_chain_rows(), _chain_accumulators(), pltpu.SMEM((1,), jnp.int32)],
        compiler_params=_params(2),
        name="sb_attn",
    )(qvt, k, qvt, tri)


def _mixout_kernel(h_ref, g_ref, oa_ref, ob_ref, oc_ref, wg_ref, pa_ref, pb_ref, pc_ref, wo_ref,
                   o_ref):
    h = h_ref[...]
    d = h.shape[1]
    u = _rms(h, g_ref[...], EPS).astype(BF16)
    m = jnp.zeros(h.shape, F32)
    for n, (o_r, p_r) in enumerate(((oa_ref, pa_ref), (ob_ref, pb_ref), (oc_ref, pc_ref))):
        gate = jax.nn.sigmoid(_dot(u, wg_ref[:, n * d:(n + 1) * d]))
        m = m + gate * _dot(o_r[...], p_r[0].astype(BF16))
    o_ref[...] = h + _dot(m.astype(BF16), wo_ref[0].astype(BF16))


def _mixout(h, g, oa, ob, oc, wg, pa, pb, pc, wo, layer):
    t, d = h.shape
    row = pl.BlockSpec((TM_DENSE, d), lambda i: (i, 0))
    orow = pl.BlockSpec((TM_DENSE, BRANCH_W), lambda i: (i, 0))
    return pl.pallas_call(
        _mixout_kernel,
        grid=(t // TM_DENSE,),
        in_specs=[row, _resident((1, d)), orow, orow, orow, _resident(wg.shape),
                  _layer_resident(pa.shape, layer), _layer_resident(pb.shape, layer),
                  _layer_resident(pc.shape, layer), _layer_resident(wo.shape, layer)],
        out_specs=row,
        out_shape=jax.ShapeDtypeStruct((t, d), F32),
        compiler_params=_params(1),
        name="mixout",
    )(h, g, oa, ob, oc, wg, pa, pb, pc, wo)


def _decay_selector():
    sel = [[0.0] * BRANCH_W for _ in range(LANES)]
    for head in range(N_FOX):
        for piece in range(N_SPLIT):
            sel[piece * BF16_ROWS + head][(head // 2) * LANES + N_SPLIT * (head % 2) + piece] = 1.0
    return jnp.asarray(sel, BF16)


def kernel(x, ffn1_norm, ffn1_w1, ffn1_w3, ffn1_w2, mix_norm, w_in, forget_bias, diff_lq1, diff_lk1,
           diff_lq2, diff_lk2, diff_subln, proj_a, proj_b, proj_c, w_out, ffn2_norm, ffn2_w1,
           ffn2_w3, ffn2_w2, final_norm):
    b, s, d = x.shape
    depth = w_in.shape[0]
    assert s % (Q_PER_STEP * T_ATTN) == 0 and (b * s) % TM_DENSE == 0 and s <= POS_RADIX * POS_RADIX
    bw = BRANCH_W
    f_lo = 6 * bw
    c_lo = f_lo + N_FOX
    g_lo = c_lo + 3 * bw

    slopes = jnp.asarray([2.0 ** (-8.0 * (hh + 1) / N_GROUPS) for hh in range(N_GROUPS)], F32)
    ridx = lax.broadcasted_iota(jnp.int32, (T_ATTN + BF16_ROWS, T_ATTN), 0)
    cidx = lax.broadcasted_iota(jnp.int32, (T_ATTN + BF16_ROWS, T_ATTN), 1)
    tri = ((cidx > ridx) | (ridx == T_ATTN)).astype(BF16)
    kpos = lax.broadcasted_iota(jnp.int32, (s, LANES), 0)
    plane = lax.broadcasted_iota(jnp.int32, (s, LANES), 1)
    pos = jnp.where(plane >= 2 * N_SPLIT, 0,
                    jnp.where(plane % 2 == 0, kpos // POS_RADIX, kpos % POS_RADIX)).astype(BF16)[None]
    sel = _decay_selector()
    row = lambda a: a.reshape(1, -1).astype(F32)

    h = x.reshape(b * s, d)
    for l in range(depth):
        h = _ffn(h, row(ffn1_norm[l]), ffn1_w1, ffn1_w3, ffn1_w2, row(final_norm), l, final=False)

        w = w_in[l]
        col = lambda lo: w[:, lo:lo + bw]
        w_k = jnp.concatenate([col(bw), col(4 * bw), col(c_lo + bw)], axis=1).astype(BF16)
        w_qvt = jnp.concatenate([col(0), col(3 * bw), col(c_lo),
                                 col(2 * bw), col(5 * bw), col(c_lo + 2 * bw)], axis=1).T.astype(BF16)
        w_ft = jnp.zeros((BF16_ROWS, d), BF16).at[:N_FOX].set(w[:, f_lo:c_lo].T.astype(BF16))
        k, qvt, ft = _inproj(h, row(mix_norm[l]), w_k, w_qvt, w_ft)
        k = k.reshape(-1, b, s, bw)

        bias = jnp.zeros((BF16_ROWS, 1), F32).at[:N_FOX, 0].set(forget_bias[l])
        kc = _decay(ft, bias, sel, b, s)

        lam_init = 0.8 - 0.6 * math.exp(-0.3 * l)
        lam_rows = jnp.zeros((8, LANES), F32).at[:4, :HEAD_DIM].set(
            jnp.stack([diff_lq1[l], diff_lk1[l], diff_lq2[l], diff_lk2[l]]))
        o_a = _diff_attn(k, qvt, pos, slopes, lam_rows, row(diff_subln[l]), lam_init)
        o_b = _fox_attn(k, qvt, kc)
        o_c = _sb_attn(k, qvt, tri)

        flat = lambda a: a.reshape(b * s, bw)
        h = _mixout(h, row(mix_norm[l]), flat(o_a), flat(o_b), flat(o_c), w[:, g_lo:].astype(BF16),
                    proj_a, proj_b, proj_c, w_out, l)

        h = _ffn(h, row(ffn2_norm[l]), ffn2_w1, ffn2_w3, ffn2_w2, row(final_norm), l,
                 final=(l == depth - 1))
    return h.reshape(b, s, d)
```
